```python
import jax, jax.numpy as jnp
from jax import lax
import numpy as np

D_MODEL = 1024
BATCH = 16
SEQ = 4096
DEPTH = 1

CHUNK = 64
D_MIX = D_MODEL
ML_WIDTH = D_MIX // 2
ML_HEADS = 4
ML_HEAD_DIM = ML_WIDTH // ML_HEADS
SB_WIDTH = D_MIX - ML_WIDTH
SB_HEADS = 8
SB_HEAD_DIM = SB_WIDTH // SB_HEADS
CONV_K = 4
Q_BLOCK = 128
EPS = 1e-6
IN_SIZES = [2 * ML_WIDTH,
            ML_WIDTH,
            ML_WIDTH,
            ML_WIDTH,
            2 * ML_HEADS,
            SB_WIDTH,
            SB_WIDTH,
            SB_WIDTH,
            SB_WIDTH]
N_IN = sum(IN_SIZES)
IN_SPLITS = [int(s) for s in np.cumsum(IN_SIZES)[:-1]]

kernel_name = "hymba_mlstm_stickbreaking_layer"


def _rmsnorm(x, g):
    xf = x.astype(jnp.float32)
    r = lax.rsqrt(jnp.mean(xf * xf, axis=-1, keepdims=True) + EPS)
    return (xf * r).astype(x.dtype) * g


def _causal_conv(u, w, b):
    C = u.shape[-1]
    out = lax.conv_general_dilated(u, w[:, None, :].astype(u.dtype), window_strides=(1,),
                                   padding=[(CONV_K - 1, 0)],
                                   dimension_numbers=('NWC', 'WIO', 'NWC'),
                                   feature_group_count=C)
    return out + b


def _mlstm(q, k, v, i_pre, f_pre):
    B, S, H, d = q.shape
    nc = S // CHUNK
    L = CHUNK
    f32 = jnp.float32

    def chunked(t):
        t = t.reshape((B, nc, L) + t.shape[2:])
        return jnp.moveaxis(t, 3, 1)

    q = chunked(q).astype(f32)
    k = chunked(k).astype(f32) * (d ** -0.5)
    v = chunked(v).astype(f32)
    ig = chunked(i_pre).astype(f32)
    logf = jax.nn.log_sigmoid(chunked(f_pre).astype(f32))
    b = jnp.cumsum(logf, axis=-1)
    a = b[..., -1]

    g = a[..., None] - b + ig
    m_loc = jnp.max(g, axis=-1)
    w = jnp.exp(g - m_loc[..., None])
    wk = w[..., None] * k
    C_loc = jnp.einsum('bhcsk,bhcsv->bhckv', wk, v)
    n_loc = jnp.sum(wk, axis=-2)

    def step(carry, xs):
        C, n, m = carry
        a_c, m_l, C_l, n_l = xs
        m_new = jnp.maximum(a_c + m, m_l)
        s_old = jnp.exp(a_c + m - m_new)
        s_loc = jnp.exp(m_l - m_new)
        C_new = s_old[..., None, None] * C + s_loc[..., None, None] * C_l
        n_new = s_old[..., None] * n + s_loc[..., None] * n_l
        return (C_new, n_new, m_new), (C, n, m)

    init = (jnp.zeros((B, H, d, d), f32), jnp.zeros((B, H, d), f32), jnp.zeros((B, H), f32))
    xs = (jnp.moveaxis(a, 2, 0), jnp.moveaxis(m_loc, 2, 0),
          jnp.moveaxis(C_loc, 2, 0), jnp.moveaxis(n_loc, 2, 0))
    _, (C_prev, n_prev, m_prev) = lax.scan(step, init, xs)
    C_prev = jnp.moveaxis(C_prev, 0, 2)
    n_prev = jnp.moveaxis(n_prev, 0, 2)
    m_prev = jnp.moveaxis(m_prev, 0, 2)

    causal = jnp.tril(jnp.ones((L, L), dtype=bool))
    Dlog = b[..., :, None] - b[..., None, :] + ig[..., None, :]
    Dlog = jnp.where(causal, Dlog, -jnp.inf)
    inter_log = b + m_prev[..., None]
    m_t = jnp.maximum(inter_log, jnp.max(Dlog, axis=-1))
    Dw = jnp.exp(Dlog - m_t[..., None])
    inter_w = jnp.exp(inter_log - m_t)
    qk = jnp.einsum('bhctd,bhcsd->bhcts', q, k) * Dw
    num = (inter_w[..., None] * jnp.einsum('bhctk,bhckv->bhctv', q, C_prev)
           + jnp.einsum('bhcts,bhcsv->bhctv', qk, v))
    den = inter_w * jnp.einsum('bhctk,bhck->bhct', q, n_prev) + jnp.sum(qk, axis=-1)
    h = num / jnp.maximum(jnp.abs(den), jnp.exp(-m_t))[..., None]
    return jnp.moveaxis(h, 1, 3).reshape(B, S, H, d)


def _stick_breaking(q, k, v):
    B, S, H, d = q.shape
    f32 = jnp.float32
    q = q.astype(f32) * (d ** -0.5)
    k = k.astype(f32)
    v = v.astype(f32)
    outs = []
    for blk in range(S // Q_BLOCK):
        t0 = blk * Q_BLOCK
        t1 = t0 + Q_BLOCK
        qb = q[:, t0:t1]
        kb = k[:, :t1]
        vb = v[:, :t1]
        z = jnp.einsum('bthd,bshd->bhts', qb, kb)
        strict = jnp.arange(t1)[None, :] < jnp.arange(t0, t1)[:, None]
        log_beta = jax.nn.log_sigmoid(z)
        log_keep = jnp.where(strict, jax.nn.log_sigmoid(-z), 0.0)
        after = lax.cumsum(log_keep, axis=3, reverse=True) - log_keep
        A = jnp.where(strict, jnp.exp(log_beta + after), 0.0)
        outs.append(jnp.einsum('bhts,bshd->bthd', A, vb))
    return jnp.concatenate(outs, axis=1)


def setup_inputs(seed: int = 0) -> dict:
    key = jax.random.key(seed)
    ks = jax.random.split(key, 14)
    f32 = jnp.float32
    x = jax.random.normal(ks[0], (BATCH, SEQ, D_MODEL), f32)
    c = jax.random.normal(ks[1], (BATCH, D_MODEL), f32)
    w_ada = jax.random.normal(ks[2], (D_MODEL, 3 * D_MODEL), f32) * (0.5 * D_MODEL ** -0.5)
    b_ada = 0.02 * jax.random.normal(ks[3], (3 * D_MODEL,), f32)
    g_pre = 1.0 + 0.02 * jax.random.normal(ks[4], (D_MODEL,), f32)
    w_in = jax.random.normal(ks[5], (D_MODEL, N_IN), f32) * (D_MODEL ** -0.5)
    b_igate = 0.1 * jax.random.normal(ks[6], (ML_HEADS,), f32)
    b_fgate = jnp.linspace(3.0, 6.0, ML_HEADS, dtype=f32) + 0.02 * jax.random.normal(ks[7], (ML_HEADS,), f32)
    conv_w = jax.random.normal(ks[8], (CONV_K, 2 * ML_WIDTH), f32) * (CONV_K ** -0.5)
    conv_b = 0.02 * jax.random.normal(ks[9], (2 * ML_WIDTH,), f32)
    g_ml_head = 1.0 + 0.02 * jax.random.normal(ks[10], (ML_HEADS, ML_HEAD_DIM), f32)
    w_out = jax.random.normal(ks[11], (D_MIX, D_MODEL), f32) * (D_MIX ** -0.5)
    g_post = 1.0 + 0.02 * jax.random.normal(ks[12], (D_MODEL,), f32)
    return {"x": x, "c": c, "w_ada": w_ada, "b_ada": b_ada, "g_pre": g_pre,
            "w_in": w_in, "b_igate": b_igate, "b_fgate": b_fgate,
            "conv_w": conv_w, "conv_b": conv_b, "g_ml_head": g_ml_head,
            "w_out": w_out, "g_post": g_post}


def reference(x, c, w_ada, b_ada, g_pre, w_in, b_igate, b_fgate, conv_w, conv_b,
              g_ml_head, w_out, g_post):
    B, S, _ = x.shape
    for _layer in range(DEPTH):
        mod = jax.nn.silu(c) @ w_ada + b_ada
        shift, scale, gate = jnp.split(mod, 3, axis=-1)
        h = _rmsnorm(x, g_pre) * (1.0 + scale[:, None, :]) + shift[:, None, :]

        proj = h @ w_in
        (ml_qk, ml_v, ml_o, ml_z, ml_gates,
         sb_q, sb_k, sb_v, sb_z) = jnp.split(proj, IN_SPLITS, axis=-1)

        ml_qk = jax.nn.silu(_causal_conv(ml_qk, conv_w, conv_b))
        ml_q, ml_k = jnp.split(ml_qk, 2, axis=-1)
        i_pre = ml_gates[..., :ML_HEADS] + b_igate
        f_pre = ml_gates[..., ML_HEADS:] + b_fgate
        hs = (B, S, ML_HEADS, ML_HEAD_DIM)
        h_ml = _mlstm(ml_q.reshape(hs), ml_k.reshape(hs), ml_v.reshape(hs), i_pre, f_pre)
        h_ml = jax.nn.sigmoid(ml_o.reshape(hs).astype(jnp.float32)) * h_ml
        h_ml = _rmsnorm(h_ml, g_ml_head).reshape(B, S, ML_WIDTH)
        h_ml = h_ml.astype(x.dtype) * jax.nn.silu(ml_z)

        ss = (B, S, SB_HEADS, SB_HEAD_DIM)
        h_sb = _stick_breaking(sb_q.reshape(ss), sb_k.reshape(ss), sb_v.reshape(ss))
        h_sb = h_sb.reshape(B, S, SB_WIDTH).astype(x.dtype) * jax.nn.silu(sb_z)

        y = jnp.concatenate([h_ml, h_sb], axis=-1) @ w_out
        x = x + gate[:, None, :] * _rmsnorm(y, g_post)
    return x
```

```python
import functools

import jax
import jax.numpy as jnp
from jax import lax
from jax.experimental import pallas as pl
from jax.experimental.pallas import tpu as pltpu

F32 = jnp.float32
BF16 = jnp.bfloat16
HIGHEST = lax.Precision.HIGHEST

EPS = 1e-6
ML_HEADS = 4
ML_HEAD_DIM = 128
ML_WIDTH = ML_HEADS * ML_HEAD_DIM
SB_HEADS = 8
SB_HEAD_DIM = 64
SB_WIDTH = SB_HEADS * SB_HEAD_DIM
SB_PAIRS = SB_HEADS // 2
CONV_K = 4
CHUNK = 64
LANES = 128
SUBLANES = 8

IN_TM = 512
ML_T = 256
SB_TQ = 256
SB_TK = 128
OUT_TM = 512
SB_SKIP = 110.0
VMEM_LIMIT = 56 * 1024 * 1024


def _sigmoid(x):
    return 1.0 / (1.0 + jnp.exp(-x))


def _softplus(x):
    return jnp.maximum(x, 0.0) + jnp.log(1.0 + jnp.exp(-jnp.abs(x)))


def _adaln_kernel(c_ref, w_ref, b_ref, o_ref):
    c = c_ref[...]
    s = c * _sigmoid(c)
    o_ref[...] = jnp.dot(s, w_ref[...], preferred_element_type=F32, precision=HIGHEST) + b_ref[...]


def _adaln(c, w_ada, b_ada):
    bsz, d = c.shape
    n = w_ada.shape[1]
    return pl.pallas_call(
        _adaln_kernel,
        grid=(n // d,),
        in_specs=[pl.BlockSpec((bsz, d), lambda j: (0, 0)),
                  pl.BlockSpec((d, d), lambda j: (0, j)),
                  pl.BlockSpec((1, d), lambda j: (0, j))],
        out_specs=pl.BlockSpec((bsz, d), lambda j: (0, j)),
        out_shape=jax.ShapeDtypeStruct((bsz, n), F32),
        name="adaln",
    )(c, w_ada, b_ada.reshape(1, n))


def _inproj_kernel(x_ref, shift_ref, scale_ref, gpre_ref, wml_ref, wsb_ref, wg_ref,
                   ml_ref, sbq_ref, sbk_ref, sbv_ref, sbz_ref, gates_ref, h_scr):
    x = x_ref[0]
    r = lax.rsqrt(jnp.mean(x * x, axis=-1, keepdims=True) + EPS)
    a = gpre_ref[...] * (1.0 + scale_ref[0])
    h_scr[...] = ((x * r) * a + shift_ref[0]).astype(BF16)

    n_ml = wml_ref.shape[1]
    for n in range(n_ml // 512):
        cols = slice(n * 512, (n + 1) * 512)
        ml_ref[0, :, cols] = jnp.dot(h_scr[...], wml_ref[:, cols],
                                     preferred_element_type=F32).astype(BF16)
    for n, dst in enumerate((sbq_ref, sbk_ref, sbv_ref, sbz_ref)):
        cols = slice(n * SB_WIDTH, (n + 1) * SB_WIDTH)
        res = jnp.dot(h_scr[...], wsb_ref[:, cols], preferred_element_type=F32).astype(BF16)
        for p in range(SB_PAIRS):
            dst[0, p] = res[:, p * LANES:(p + 1) * LANES]
    gates_ref[0] = jnp.dot(h_scr[...], wg_ref[...], preferred_element_type=F32)


def _inproj(x, mod3, g_pre, w_ml, w_sb, w_g):
    bsz, s, d = x.shape
    tm = min(IN_TM, s)
    n_ml = w_ml.shape[1]
    sb_shape = jax.ShapeDtypeStruct((bsz, SB_PAIRS, s, LANES), BF16)
    sb_spec = pl.BlockSpec((1, SB_PAIRS, tm, LANES), lambda b, i: (b, 0, i, 0))
    return pl.pallas_call(
        _inproj_kernel,
        grid=(bsz, s // tm),
        in_specs=[pl.BlockSpec((1, tm, d), lambda b, i: (b, i, 0)),
                  pl.BlockSpec((1, 1, d), lambda b, i: (b, 0, 0)),
                  pl.BlockSpec((1, 1, d), lambda b, i: (b, 0, 1)),
                  pl.BlockSpec((1, d), lambda b, i: (0, 0)),
                  pl.BlockSpec((d, n_ml), lambda b, i: (0, 0)),
                  pl.BlockSpec((d, 4 * SB_WIDTH), lambda b, i: (0, 0)),
                  pl.BlockSpec((d, LANES), lambda b, i: (0, 0))],
        out_specs=[pl.BlockSpec((1, tm, n_ml), lambda b, i: (b, i, 0)),
                   sb_spec, sb_spec, sb_spec, sb_spec,
                   pl.BlockSpec((1, tm, LANES), lambda b, i: (b, i, 0))],
        out_shape=[jax.ShapeDtypeStruct((bsz, s, n_ml), BF16),
                   sb_shape, sb_shape, sb_shape, sb_shape,
                   jax.ShapeDtypeStruct((bsz, s, LANES), F32)],
        scratch_shapes=[pltpu.VMEM((tm, d), BF16)],
        compiler_params=pltpu.CompilerParams(
            dimension_semantics=("parallel", "parallel"), vmem_limit_bytes=VMEM_LIMIT),
        name="inproj",
    )(x, mod3, mod3, g_pre.reshape(1, d), w_ml, w_sb, w_g)


def _mlstm_kernel(u_ref, v_ref, o_ref, z_ref, g_ref, convw_ref, convb_ref, gbias_ref, ghead_ref,
                  out_ref, ubuf, q_scr, k_scr, c_scr, n_scr, m_scr):
    t = u_ref.shape[1]
    j = pl.program_id(1)

    @pl.when(j == 0)
    def _():
        ubuf[0:SUBLANES, :] = jnp.zeros((SUBLANES, ubuf.shape[1]), F32)
        c_scr[...] = jnp.zeros_like(c_scr)
        n_scr[...] = jnp.zeros_like(n_scr)
        m_scr[...] = jnp.zeros_like(m_scr)

    @pl.when(j > 0)
    def _():
        ubuf[0:SUBLANES, :] = ubuf[t:t + SUBLANES, :]

    ubuf[SUBLANES:SUBLANES + t, :] = u_ref[0].astype(F32)

    conv = convb_ref[...]
    for i in range(CONV_K):
        off = SUBLANES - (CONV_K - 1) + i
        conv = conv + convw_ref[i:i + 1, :] * ubuf[off:off + t, :]
    qk = conv * _sigmoid(conv)
    q_scr[...] = qk[:, :ML_WIDTH]
    k_scr[...] = qk[:, ML_WIDTH:] * (ML_HEAD_DIM ** -0.5)

    gates = g_ref[0] + gbias_ref[...]
    logf = -_softplus(-gates)
    row = lax.broadcasted_iota(jnp.int32, (t, t), 0)
    col = lax.broadcasted_iota(jnp.int32, (t, t), 1)
    tri = jnp.where((row >= col) & ((row // CHUNK) == (col // CHUNK)), 1.0, 0.0).astype(F32)
    bcum = jnp.dot(tri, logf, preferred_element_type=F32, precision=HIGHEST)
    gates_t = gates.T
    bcum_t = bcum.T

    trow = lax.broadcasted_iota(jnp.int32, (CHUNK, CHUNK), 0)
    tcol = lax.broadcasted_iota(jnp.int32, (CHUNK, CHUNK), 1)
    causal = trow >= tcol

    for c in range(t // CHUNK):
        rows = slice(c * CHUNK, (c + 1) * CHUNK)
        for h in range(ML_HEADS):
            lanes = slice(h * ML_HEAD_DIM, (h + 1) * ML_HEAD_DIM)
            b_col = bcum[rows, ML_HEADS + h:ML_HEADS + h + 1]
            ig_col = gates[rows, h:h + 1]
            b_row = bcum_t[ML_HEADS + h:ML_HEADS + h + 1, rows]
            ig_row = gates_t[h:h + 1, rows]
            a = b_col[CHUNK - 1:CHUNK, :]
            q_c = q_scr[rows, lanes]
            k_c = k_scr[rows, lanes]
            v_c = v_ref[0, rows, lanes]
            m_prev = m_scr[h:h + 1, 0:1]
            c_prev = c_scr[h]
            n_prev = n_scr[h:h + 1, :]
            q_b = q_c.astype(BF16)
            k_b = k_c.astype(BF16)

            dlog = jnp.where(causal, b_col - b_row + ig_row, -jnp.inf)
            inter_log = b_col + m_prev
            m_t = jnp.maximum(inter_log, jnp.max(dlog, axis=1, keepdims=True))
            dw = jnp.exp(dlog - m_t)
            inter_w = jnp.exp(inter_log - m_t)
            s_qk = lax.dot_general(q_b, k_b, (((1,), (1,)), ((), ())), preferred_element_type=F32)
            qkm = s_qk * dw
            num = (inter_w * jnp.dot(q_b, c_prev.astype(BF16), preferred_element_type=F32)
                   + jnp.dot(qkm.astype(BF16), v_c, preferred_element_type=F32))
            den = (inter_w * jnp.sum(q_c * n_prev, axis=1, keepdims=True)
                   + jnp.sum(qkm, axis=1, keepdims=True))
            hh = num / jnp.maximum(jnp.abs(den), jnp.exp(-m_t))

            g_col = a - b_col + ig_col
            g_row = a - b_row + ig_row
            m_loc = jnp.max(g_row, axis=1, keepdims=True)
            wk = jnp.exp(g_col - m_loc) * k_c
            c_loc = lax.dot_general(wk.astype(BF16), v_c, (((0,), (0,)), ((), ())),
                                    preferred_element_type=F32)
            n_loc = jnp.sum(wk, axis=0, keepdims=True)
            m_new = jnp.maximum(a + m_prev, m_loc)
            s_old = jnp.exp(a + m_prev - m_new)
            s_loc = jnp.exp(m_loc - m_new)
            c_scr[h] = s_old * c_prev + s_loc * c_loc
            n_scr[h:h + 1, :] = s_old * n_prev + s_loc * n_loc
            m_scr[h:h + 1, :] = jnp.broadcast_to(m_new, (1, LANES))

            hg = _sigmoid(o_ref[0, rows, lanes].astype(F32)) * hh
            hn = hg * lax.rsqrt(jnp.mean(hg * hg, axis=1, keepdims=True) + EPS) * ghead_ref[:, lanes]
            zc = z_ref[0, rows, lanes].astype(F32)
            out_ref[0, rows, lanes] = (hn * (zc * _sigmoid(zc))).astype(BF16)


def _mlstm(ml, gates, conv_w, conv_b, gbias, g_head):
    bsz, s, _ = ml.shape
    t = min(ML_T, s)
    w2 = 2 * ML_WIDTH
    return pl.pallas_call(
        _mlstm_kernel,
        grid=(bsz, s // t),
        in_specs=[pl.BlockSpec((1, t, w2), lambda b, j: (b, j, 0)),
                  pl.BlockSpec((1, t, ML_WIDTH), lambda b, j: (b, j, 2)),
                  pl.BlockSpec((1, t, ML_WIDTH), lambda b, j: (b, j, 3)),
                  pl.BlockSpec((1, t, ML_WIDTH), lambda b, j: (b, j, 4)),
                  pl.BlockSpec((1, t, LANES), lambda b, j: (b, j, 0)),
                  pl.BlockSpec((CONV_K, w2), lambda b, j: (0, 0)),
                  pl.BlockSpec((1, w2), lambda b, j: (0, 0)),
                  pl.BlockSpec((1, LANES), lambda b, j: (0, 0)),
                  pl.BlockSpec((1, ML_WIDTH), lambda b, j: (0, 0))],
        out_specs=pl.BlockSpec((1, t, ML_WIDTH), lambda b, j: (b, j, 0)),
        out_shape=jax.ShapeDtypeStruct((bsz, s, ML_WIDTH), BF16),
        scratch_shapes=[pltpu.VMEM((t + 2 * SUBLANES, w2), F32),
                        pltpu.VMEM((t, ML_WIDTH), F32),
                        pltpu.VMEM((t, ML_WIDTH), F32),
                        pltpu.VMEM((ML_HEADS, ML_HEAD_DIM, ML_HEAD_DIM), F32),
                        pltpu.VMEM((SUBLANES, ML_HEAD_DIM), F32),
                        pltpu.VMEM((SUBLANES, LANES), F32)],
        compiler_params=pltpu.CompilerParams(
            dimension_semantics=("parallel", "arbitrary"), vmem_limit_bytes=VMEM_LIMIT),
        name="mlstm",
    )(ml, ml, ml, ml, gates, conv_w, conv_b.reshape(1, w2), gbias, g_head.reshape(1, ML_WIDTH))


def _sb_tile(kt, p, t0, q2, k_ref, v_ref, cum_ref, acc_scr, r_scr, masked):
    tq = q2.shape[0]
    half = lax.broadcasted_iota(jnp.int32, (SB_TK, LANES), 1) < SB_HEAD_DIM
    kk = k_ref[0, p, pl.ds(kt * SB_TK, SB_TK), :]
    vv = v_ref[0, p, pl.ds(kt * SB_TK, SB_TK), :]
    zero = jnp.zeros_like(kk)
    k_bd = jnp.concatenate([jnp.where(half, kk, zero), jnp.where(half, zero, kk)], axis=0)
    v_bd = jnp.concatenate([jnp.where(half, vv, zero), jnp.where(half, zero, vv)], axis=0)
    z = lax.dot_general(q2, k_bd, (((1,), (1,)), ((), ())), preferred_element_type=F32)
    nlk = _softplus(z)
    if masked:
        key = kt * SB_TK + (lax.broadcasted_iota(jnp.int32, (tq, 2 * SB_TK), 1) & (SB_TK - 1))
        qry = t0 + lax.broadcasted_iota(jnp.int32, (tq, 2 * SB_TK), 0)
        strict = key < qry
        nlk = jnp.where(strict, nlk, 0.0)
    hi = nlk.astype(BF16)
    lo = (nlk - hi.astype(F32)).astype(BF16)
    m = jnp.dot(jnp.concatenate([hi, lo], axis=1), cum_ref[...], preferred_element_type=F32)
    a = jnp.exp(z - m)
    if masked:
        a = jnp.where(strict, a, 0.0)
    pv = jnp.dot(a.astype(BF16), v_bd, preferred_element_type=F32)
    r = r_scr[...]
    acc_scr[...] += jnp.exp(-r) * pv
    lane_a = lax.broadcasted_iota(jnp.int32, (tq, LANES), 1) < SB_HEAD_DIM
    r_new = r + jnp.where(lane_a, m[:, 0:1], m[:, SB_TK:SB_TK + 1])
    r_scr[...] = r_new
    return jnp.min(r_new)


def _sb_kernel(q_ref, k_ref, v_ref, z_ref, cum_ref, o_ref, acc_scr, r_scr):
    tq = q_ref.shape[2]
    i = pl.program_id(1)
    t0 = i * tq
    kt_diag = (t0 + tq) // SB_TK - 1

    def pair_body(p, carry):
        q2 = q_ref[0, p] * jnp.asarray(SB_HEAD_DIM ** -0.5, BF16)
        acc_scr[...] = jnp.zeros_like(acc_scr)
        r_scr[...] = jnp.zeros_like(r_scr)
        rmin = jnp.float32(0.0)
        for d in range(tq // SB_TK):
            rmin = _sb_tile(kt_diag - d, p, t0, q2, k_ref, v_ref, cum_ref, acc_scr, r_scr, True)

        def cond(c):
            kt, rm = c
            return jnp.logical_and(kt >= 0, rm < SB_SKIP)

        def body(c):
            kt, _ = c
            rm = _sb_tile(kt, p, t0, q2, k_ref, v_ref, cum_ref, acc_scr, r_scr, False)
            return kt - 1, rm

        lax.while_loop(cond, body, (kt_diag - tq // SB_TK, rmin))
        zc = z_ref[0, p].astype(F32)
        o_ref[0, p] = (acc_scr[...] * (zc * _sigmoid(zc))).astype(BF16)
        return carry

    lax.fori_loop(0, SB_PAIRS, pair_body, 0)


def _stick_breaking(sbq, sbk, sbv, sbz):
    bsz, pairs, s, _ = sbq.shape
    tq = min(SB_TQ, s)
    j = lax.broadcasted_iota(jnp.int32, (2 * SB_TK, 2 * SB_TK), 0)
    c = lax.broadcasted_iota(jnp.int32, (2 * SB_TK, 2 * SB_TK), 1)
    blk = jnp.where((j >= c) & ((j // SB_TK) == (c // SB_TK)), 1.0, 0.0).astype(BF16)
    cum = jnp.concatenate([blk, blk], axis=0)
    q_spec = pl.BlockSpec((1, pairs, tq, LANES), lambda b, i: (b, 0, i, 0))
    kv_spec = pl.BlockSpec((1, pairs, s, LANES), lambda b, i: (b, 0, 0, 0))
    return pl.pallas_call(
        _sb_kernel,
        grid=(bsz, s // tq),
        in_specs=[q_spec, kv_spec, kv_spec, q_spec,
                  pl.BlockSpec((4 * SB_TK, 2 * SB_TK), lambda b, i: (0, 0))],
        out_specs=q_spec,
        out_shape=jax.ShapeDtypeStruct((bsz, pairs, s, LANES), BF16),
        scratch_shapes=[pltpu.VMEM((tq, LANES), F32), pltpu.VMEM((tq, LANES), F32)],
        compiler_params=pltpu.CompilerParams(
            dimension_semantics=("parallel", "arbitrary"), vmem_limit_bytes=VMEM_LIMIT),
        name="stickbreak",
    )(sbq, sbk, sbv, sbz, cum)


def _outproj_kernel(hml_ref, hsb_ref, x_ref, gate_ref, gpost_ref, wml_ref, wsb_ref, o_ref):
    hsb = jnp.concatenate([hsb_ref[0, p] for p in range(SB_PAIRS)], axis=1)
    y = (jnp.dot(hml_ref[0], wml_ref[...], preferred_element_type=F32)
         + jnp.dot(hsb, wsb_ref[...], preferred_element_type=F32))
    r = lax.rsqrt(jnp.mean(y * y, axis=-1, keepdims=True) + EPS)
    o_ref[0] = x_ref[0] + gate_ref[0] * ((y * r) * gpost_ref[...])


def _outproj(hml, hsb, x, mod3, g_post, w_oml, w_osb):
    bsz, s, d = x.shape
    tm = min(OUT_TM, s)
    return pl.pallas_call(
        _outproj_kernel,
        grid=(bsz, s // tm),
        in_specs=[pl.BlockSpec((1, tm, ML_WIDTH), lambda b, i: (b, i, 0)),
                  pl.BlockSpec((1, SB_PAIRS, tm, LANES), lambda b, i: (b, 0, i, 0)),
                  pl.BlockSpec((1, tm, d), lambda b, i: (b, i, 0)),
                  pl.BlockSpec((1, 1, d), lambda b, i: (b, 0, 2)),
                  pl.BlockSpec((1, d), lambda b, i: (0, 0)),
                  pl.BlockSpec((ML_WIDTH, d), lambda b, i: (0, 0)),
                  pl.BlockSpec((SB_WIDTH, d), lambda b, i: (0, 0))],
        out_specs=pl.BlockSpec((1, tm, d), lambda b, i: (b, i, 0)),
        out_shape=jax.ShapeDtypeStruct((bsz, s, d), F32),
        compiler_params=pltpu.CompilerParams(
            dimension_semantics=("parallel", "parallel"), vmem_limit_bytes=VMEM_LIMIT),
        name="outproj",
    )(hml, hsb, x, mod3, g_post.reshape(1, d), w_oml, w_osb)


def kernel(x, c, w_ada, b_ada, g_pre, w_in, b_igate, b_fgate, conv_w, conv_b, g_ml_head, w_out, g_post):
    bsz, s, d = x.shape
    o_gate = 2 * ML_WIDTH + 3 * ML_WIDTH
    o_sb = o_gate + 2 * ML_HEADS
    w_ml = w_in[:, :o_gate].astype(BF16)
    w_g = jnp.pad(w_in[:, o_gate:o_sb], ((0, 0), (0, LANES - 2 * ML_HEADS))).astype(BF16)
    w_sb = w_in[:, o_sb:].astype(BF16)
    gbias = jnp.pad(jnp.concatenate([b_igate, b_fgate]), (0, LANES - 2 * ML_HEADS)).reshape(1, LANES)
    w_o = w_out.astype(BF16)

    mod3 = _adaln(c, w_ada, b_ada).reshape(bsz, 1, 3 * d)
    ml, sbq, sbk, sbv, sbz, gates = _inproj(x, mod3, g_pre, w_ml, w_sb, w_g)
    hml = _mlstm(ml, gates, conv_w, conv_b, gbias, g_ml_head)
    hsb = _stick_breaking(sbq, sbk, sbv, sbz)
    return _outproj(hml, hsb, x, mod3, g_post, w_o[:ML_WIDTH], w_o[ML_WIDTH:])
```

```python
import functools

import jax
import jax.numpy as jnp
from jax import lax
from jax.experimental import pallas as pl
from jax.experimental.pallas import tpu as pltpu

F32 = jnp.float32
BF16 = jnp.bfloat16
HIGHEST = lax.Precision.HIGHEST

EPS = 1e-6
ML_HEADS = 4
ML_HEAD_DIM = 128
ML_WIDTH = ML_HEADS * ML_HEAD_DIM
SB_HEADS = 8
SB_HEAD_DIM = 64
SB_WIDTH = SB_HEADS * SB_HEAD_DIM
SB_PAIRS = SB_HEADS // 2
CONV_K = 4
CHUNK = 64
LANES = 128
SUBLANES = 8

IN_TM = 512
ML_T = 256
SB_TQ = 128
SB_TK = 128
SB_ROWS = 256
OUT_TM = 512
LOG2E = 1.4426950408889634
SB_SKIP = 160.0
SB_DEAD = 1e30
SB_ZMAX = 126.0
VMEM_LIMIT = 56 * 1024 * 1024


def _sigmoid(x):
    return 1.0 / (1.0 + jnp.exp(-x))


def _softplus(x):
    return jnp.maximum(x, 0.0) + jnp.log(1.0 + jnp.exp(-jnp.abs(x)))


def _adaln_kernel(c_ref, w_ref, b_ref, o_ref):
    c = c_ref[...]
    s = c * _sigmoid(c)
    o_ref[...] = jnp.dot(s, w_ref[...], preferred_element_type=F32, precision=HIGHEST) + b_ref[...]


def _adaln(c, w_ada, b_ada):
    bsz, d = c.shape
    n = w_ada.shape[1]
    return pl.pallas_call(
        _adaln_kernel,
        grid=(n // d,),
        in_specs=[pl.BlockSpec((bsz, d), lambda j: (0, 0)),
                  pl.BlockSpec((d, d), lambda j: (0, j)),
                  pl.BlockSpec((1, d), lambda j: (0, j))],
        out_specs=pl.BlockSpec((bsz, d), lambda j: (0, j)),
        out_shape=jax.ShapeDtypeStruct((bsz, n), F32),
        name="adaln",
    )(c, w_ada, b_ada.reshape(1, n))


def _inproj_kernel(x_ref, shift_ref, scale_ref, gpre_ref, wml_ref, wsb_ref, wg_ref,
                   ml_ref, sbq_ref, sbk_ref, sbv_ref, sbz_ref, gates_ref, h_scr):
    x = x_ref[0]
    r = lax.rsqrt(jnp.mean(x * x, axis=-1, keepdims=True) + EPS)
    a = gpre_ref[...] * (1.0 + scale_ref[0])
    h_scr[...] = ((x * r) * a + shift_ref[0]).astype(BF16)

    n_ml = wml_ref.shape[1]
    for n in range(n_ml // 512):
        cols = slice(n * 512, (n + 1) * 512)
        ml_ref[0, :, cols] = jnp.dot(h_scr[...], wml_ref[:, cols],
                                     preferred_element_type=F32).astype(BF16)
    tm = x.shape[0]
    half = lax.broadcasted_iota(jnp.int32, (tm, LANES), 1) < SB_HEAD_DIM
    for n, dst in enumerate((sbq_ref, sbk_ref, sbv_ref, sbz_ref)):
        cols = slice(n * SB_WIDTH, (n + 1) * SB_WIDTH)
        res = jnp.dot(h_scr[...], wsb_ref[:, cols], preferred_element_type=F32).astype(BF16)
        for p in range(SB_PAIRS):
            blk = res[:, p * LANES:(p + 1) * LANES]
            if dst is sbk_ref or dst is sbv_ref:
                zero = jnp.zeros_like(blk)
                only_a = jnp.where(half, blk, zero)
                only_b = jnp.where(half, zero, blk)
                for j in range(tm // SB_TK):
                    rows = slice(j * SB_TK, (j + 1) * SB_TK)
                    dst[0, p, 2 * j * SB_TK:(2 * j + 1) * SB_TK, :] = only_a[rows]
                    dst[0, p, (2 * j + 1) * SB_TK:(2 * j + 2) * SB_TK, :] = only_b[rows]
            else:
                dst[0, p] = blk
    gates_ref[0] = jnp.dot(h_scr[...], wg_ref[...], preferred_element_type=F32)


def _inproj(x, mod3, g_pre, w_ml, w_sb, w_g):
    bsz, s, d = x.shape
    tm = min(IN_TM, s)
    n_ml = w_ml.shape[1]
    sb_shape = jax.ShapeDtypeStruct((bsz, SB_PAIRS, s, LANES), BF16)
    sb_spec = pl.BlockSpec((1, SB_PAIRS, tm, LANES), lambda b, i: (b, 0, i, 0))
    kv_shape = jax.ShapeDtypeStruct((bsz, SB_PAIRS, 2 * s, LANES), BF16)
    kv_spec = pl.BlockSpec((1, SB_PAIRS, 2 * tm, LANES), lambda b, i: (b, 0, i, 0))
    return pl.pallas_call(
        _inproj_kernel,
        grid=(bsz, s // tm),
        in_specs=[pl.BlockSpec((1, tm, d), lambda b, i: (b, i, 0)),
                  pl.BlockSpec((1, 1, d), lambda b, i: (b, 0, 0)),
                  pl.BlockSpec((1, 1, d), lambda b, i: (b, 0, 1)),
                  pl.BlockSpec((1, d), lambda b, i: (0, 0)),
                  pl.BlockSpec((d, n_ml), lambda b, i: (0, 0)),
                  pl.BlockSpec((d, 4 * SB_WIDTH), lambda b, i: (0, 0)),
                  pl.BlockSpec((d, LANES), lambda b, i: (0, 0))],
        out_specs=[pl.BlockSpec((1, tm, n_ml), lambda b, i: (b, i, 0)),
                   sb_spec, kv_spec, kv_spec, sb_spec,
                   pl.BlockSpec((1, tm, LANES), lambda b, i: (b, i, 0))],
        out_shape=[jax.ShapeDtypeStruct((bsz, s, n_ml), BF16),
                   sb_shape, kv_shape, kv_shape, sb_shape,
                   jax.ShapeDtypeStruct((bsz, s, LANES), F32)],
        scratch_shapes=[pltpu.VMEM((tm, d), BF16)],
        compiler_params=pltpu.CompilerParams(
            dimension_semantics=("parallel", "parallel"), vmem_limit_bytes=VMEM_LIMIT),
        name="inproj",
    )(x, mod3, mod3, g_pre.reshape(1, d), w_ml, w_sb, w_g)


def _mlstm_kernel(u_ref, v_ref, o_ref, z_ref, g_ref, convw_ref, convb_ref, gbias_ref, ghead_ref,
                  out_ref, ubuf, q_scr, k_scr, c_scr, n_scr, m_scr):
    t = u_ref.shape[1]
    j = pl.program_id(1)

    @pl.when(j == 0)
    def _():
        ubuf[0:SUBLANES, :] = jnp.zeros((SUBLANES, ubuf.shape[1]), F32)
        c_scr[...] = jnp.zeros_like(c_scr)
        n_scr[...] = jnp.zeros_like(n_scr)
        m_scr[...] = jnp.zeros_like(m_scr)

    @pl.when(j > 0)
    def _():
        ubuf[0:SUBLANES, :] = ubuf[t:t + SUBLANES, :]

    ubuf[SUBLANES:SUBLANES + t, :] = u_ref[0].astype(F32)

    conv = convb_ref[...]
    for i in range(CONV_K):
        off = SUBLANES - (CONV_K - 1) + i
        conv = conv + convw_ref[i:i + 1, :] * ubuf[off:off + t, :]
    qk = conv * _sigmoid(conv)
    q_scr[...] = qk[:, :ML_WIDTH]
    k_scr[...] = qk[:, ML_WIDTH:] * (ML_HEAD_DIM ** -0.5)

    gates = g_ref[0] + gbias_ref[...]
    logf = -_softplus(-gates)
    row = lax.broadcasted_iota(jnp.int32, (t, t), 0)
    col = lax.broadcasted_iota(jnp.int32, (t, t), 1)
    tri = jnp.where((row >= col) & ((row // CHUNK) == (col // CHUNK)), 1.0, 0.0).astype(F32)
    bcum = jnp.dot(tri, logf, preferred_element_type=F32, precision=HIGHEST)
    gates_t = gates.T
    bcum_t = bcum.T

    trow = lax.broadcasted_iota(jnp.int32, (CHUNK, CHUNK), 0)
    tcol = lax.broadcasted_iota(jnp.int32, (CHUNK, CHUNK), 1)
    causal = trow >= tcol

    for c in range(t // CHUNK):
        rows = slice(c * CHUNK, (c + 1) * CHUNK)
        for h in range(ML_HEADS):
            lanes = slice(h * ML_HEAD_DIM, (h + 1) * ML_HEAD_DIM)
            b_col = bcum[rows, ML_HEADS + h:ML_HEADS + h + 1]
            ig_col = gates[rows, h:h + 1]
            b_row = bcum_t[ML_HEADS + h:ML_HEADS + h + 1, rows]
            ig_row = gates_t[h:h + 1, rows]
            a = b_col[CHUNK - 1:CHUNK, :]
            q_c = q_scr[rows, lanes]
            k_c = k_scr[rows, lanes]
            v_c = v_ref[0, rows, lanes]
            m_prev = m_scr[h:h + 1, 0:1]
            c_prev = c_scr[h]
            n_prev = n_scr[h:h + 1, :]
            q_b = q_c.astype(BF16)
            k_b = k_c.astype(BF16)

            dlog = jnp.where(causal, b_col - b_row + ig_row, -jnp.inf)
            inter_log = b_col + m_prev
            m_t = jnp.maximum(inter_log, jnp.max(dlog, axis=1, keepdims=True))
            dw = jnp.exp(dlog - m_t)
            inter_w = jnp.exp(inter_log - m_t)
            s_qk = lax.dot_general(q_b, k_b, (((1,), (1,)), ((), ())), preferred_element_type=F32)
            qkm = s_qk * dw
            num = (inter_w * jnp.dot(q_b, c_prev.astype(BF16), preferred_element_type=F32)
                   + jnp.dot(qkm.astype(BF16), v_c, preferred_element_type=F32))
            den = (inter_w * jnp.sum(q_c * n_prev, axis=1, keepdims=True)
                   + jnp.sum(qkm, axis=1, keepdims=True))
            hh = num / jnp.maximum(jnp.abs(den), jnp.exp(-m_t))

            g_col = a - b_col + ig_col
            g_row = a - b_row + ig_row
            m_loc = jnp.max(g_row, axis=1, keepdims=True)
            wk = jnp.exp(g_col - m_loc) * k_c
            c_loc = lax.dot_general(wk.astype(BF16), v_c, (((0,), (0,)), ((), ())),
                                    preferred_element_type=F32)
            n_loc = jnp.sum(wk, axis=0, keepdims=True)
            m_new = jnp.maximum(a + m_prev, m_loc)
            s_old = jnp.exp(a + m_prev - m_new)
            s_loc = jnp.exp(m_loc - m_new)
            c_scr[h] = s_old * c_prev + s_loc * c_loc
            n_scr[h:h + 1, :] = s_old * n_prev + s_loc * n_loc
            m_scr[h:h + 1, :] = jnp.broadcast_to(m_new, (1, LANES))

            hg = _sigmoid(o_ref[0, rows, lanes].astype(F32)) * hh
            hn = hg * lax.rsqrt(jnp.mean(hg * hg, axis=1, keepdims=True) + EPS) * ghead_ref[:, lanes]
            zc = z_ref[0, rows, lanes].astype(F32)
            out_ref[0, rows, lanes] = (hn * (zc * _sigmoid(zc))).astype(BF16)


def _mlstm(ml, gates, conv_w, conv_b, gbias, g_head):
    bsz, s, _ = ml.shape
    t = min(ML_T, s)
    w2 = 2 * ML_WIDTH
    return pl.pallas_call(
        _mlstm_kernel,
        grid=(bsz, s // t),
        in_specs=[pl.BlockSpec((1, t, w2), lambda b, j: (b, j, 0)),
                  pl.BlockSpec((1, t, ML_WIDTH), lambda b, j: (b, j, 2)),
                  pl.BlockSpec((1, t, ML_WIDTH), lambda b, j: (b, j, 3)),
                  pl.BlockSpec((1, t, ML_WIDTH), lambda b, j: (b, j, 4)),
                  pl.BlockSpec((1, t, LANES), lambda b, j: (b, j, 0)),
                  pl.BlockSpec((CONV_K, w2), lambda b, j: (0, 0)),
                  pl.BlockSpec((1, w2), lambda b, j: (0, 0)),
                  pl.BlockSpec((1, LANES), lambda b, j: (0, 0)),
                  pl.BlockSpec((1, ML_WIDTH), lambda b, j: (0, 0))],
        out_specs=pl.BlockSpec((1, t, ML_WIDTH), lambda b, j: (b, j, 0)),
        out_shape=jax.ShapeDtypeStruct((bsz, s, ML_WIDTH), BF16),
        scratch_shapes=[pltpu.VMEM((t + 2 * SUBLANES, w2), F32),
                        pltpu.VMEM((t, ML_WIDTH), F32),
                        pltpu.VMEM((t, ML_WIDTH), F32),
                        pltpu.VMEM((ML_HEADS, ML_HEAD_DIM, ML_HEAD_DIM), F32),
                        pltpu.VMEM((SUBLANES, ML_HEAD_DIM), F32),
                        pltpu.VMEM((SUBLANES, LANES), F32)],
        compiler_params=pltpu.CompilerParams(
            dimension_semantics=("parallel", "arbitrary"), vmem_limit_bytes=VMEM_LIMIT),
        name="mlstm",
    )(ml, ml, ml, ml, gates, conv_w, conv_b.reshape(1, w2), gbias, g_head.reshape(1, ML_WIDTH))


def _sb_sweep(d, diag, qt0, chains, q_ref, k_ref, v_ref, cum_ref, z_scr, n_scr, a_scr, s_scr,
              acc_scr, r_scr):
    if diag:
        strict = ((lax.broadcasted_iota(jnp.int32, (SB_TQ, 2 * SB_TK), 1) & (SB_TK - 1))
                  < lax.broadcasted_iota(jnp.int32, (SB_TQ, 2 * SB_TK), 0))
    lane_a = lax.broadcasted_iota(jnp.int32, (SB_TQ, LANES), 1) < SB_HEAD_DIM

    def key_rows(g):
        return pl.ds(jnp.maximum(qt0 + g - d, 0) * (2 * SB_TK), 2 * SB_TK)

    for c, (g, p) in enumerate(chains):
        q2 = q_ref[0, p, g * SB_TQ:(g + 1) * SB_TQ, :]
        z = lax.dot_general(q2, k_ref[0, p, key_rows(g), :], (((1,), (1,)), ((), ())),
                            preferred_element_type=F32)
        nlk = jnp.log(1.0 + jnp.exp2(jnp.minimum(z, SB_ZMAX))) * LOG2E
        if diag:
            nlk = jnp.where(strict, nlk, 0.0)
        z_scr[c] = z
        n_scr[c] = nlk.astype(BF16)

    for c, (g, p) in enumerate(chains):
        m = jnp.dot(n_scr[c], cum_ref[...], preferred_element_type=F32)
        a = jnp.exp2(z_scr[c] - m)
        if diag:
            a = jnp.where(strict, a, 0.0)
        a_scr[c] = a.astype(BF16)
        s_scr[c] = jnp.where(lane_a, m[:, 0:1], m[:, SB_TK:SB_TK + 1])

    rmin = None
    for c, (g, p) in enumerate(chains):
        pv = jnp.dot(a_scr[c], v_ref[0, p, key_rows(g), :], preferred_element_type=F32)
        if diag:
            acc_scr[c] = pv
            r_new = s_scr[c]
        else:
            r = jnp.where(qt0 + g - d >= 0, r_scr[c], SB_DEAD)
            acc_scr[c] += jnp.exp2(-r) * pv
            r_new = r + s_scr[c]
        r_scr[c] = r_new
        rmin = r_new if rmin is None else jnp.minimum(rmin, r_new)
    return jnp.min(rmin)


def _sb_kernel(q_ref, k_ref, v_ref, z_ref, cum_ref, o_ref, z_scr, n_scr, a_scr, s_scr, acc_scr, r_scr):
    n_sub = q_ref.shape[2] // SB_TQ
    qt0 = pl.program_id(1) * n_sub
    chains = [(g, p) for g in range(n_sub) for p in range(SB_PAIRS)]
    sweep = functools.partial(_sb_sweep, qt0=qt0, chains=chains, q_ref=q_ref, k_ref=k_ref,
                              v_ref=v_ref, cum_ref=cum_ref, z_scr=z_scr, n_scr=n_scr, a_scr=a_scr,
                              s_scr=s_scr, acc_scr=acc_scr, r_scr=r_scr)

    def cond(carry):
        d, rm = carry
        return jnp.logical_and(qt0 + (n_sub - 1) - d >= 0, rm < SB_SKIP)

    def body(carry):
        d, _ = carry
        return d + 1, sweep(d, False)

    lax.while_loop(cond, body, (jnp.int32(1), sweep(0, True)))

    for c, (g, p) in enumerate(chains):
        zc = z_ref[0, p, g * SB_TQ:(g + 1) * SB_TQ, :].astype(F32)
        o_ref[0, p, g * SB_TQ:(g + 1) * SB_TQ, :] = (acc_scr[c] * (zc * _sigmoid(zc))).astype(BF16)


def _stick_breaking(sbq, sbk, sbv, sbz):
    bsz, pairs, s, _ = sbq.shape
    rows = min(SB_ROWS, s)
    j = lax.broadcasted_iota(jnp.int32, (2 * SB_TK, 2 * SB_TK), 0)
    c = lax.broadcasted_iota(jnp.int32, (2 * SB_TK, 2 * SB_TK), 1)
    cum = jnp.where((j >= c) & ((j // SB_TK) == (c // SB_TK)), 1.0, 0.0).astype(BF16)
    q_spec = pl.BlockSpec((1, pairs, rows, LANES), lambda b, i: (b, 0, i, 0))
    kv_spec = pl.BlockSpec((1, pairs, 2 * s, LANES), lambda b, i: (b, 0, 0, 0))
    n_chain = (rows // SB_TQ) * pairs
    return pl.pallas_call(
        _sb_kernel,
        grid=(bsz, s // rows),
        in_specs=[q_spec, kv_spec, kv_spec, q_spec,
                  pl.BlockSpec((2 * SB_TK, 2 * SB_TK), lambda b, i: (0, 0))],
        out_specs=q_spec,
        out_shape=jax.ShapeDtypeStruct((bsz, pairs, s, LANES), BF16),
        scratch_shapes=[pltpu.VMEM((n_chain, SB_TQ, 2 * SB_TK), F32),
                        pltpu.VMEM((n_chain, SB_TQ, 2 * SB_TK), BF16),
                        pltpu.VMEM((n_chain, SB_TQ, 2 * SB_TK), BF16),
                        pltpu.VMEM((n_chain, SB_TQ, LANES), F32),
                        pltpu.VMEM((n_chain, SB_TQ, LANES), F32),
                        pltpu.VMEM((n_chain, SB_TQ, LANES), F32)],
        compiler_params=pltpu.CompilerParams(
            dimension_semantics=("parallel", "arbitrary"), vmem_limit_bytes=VMEM_LIMIT),
        name="stickbreak",
    )(sbq, sbk, sbv, sbz, cum)


def _outproj_kernel(hml_ref, hsb_ref, x_ref, gate_ref, gpost_ref, wml_ref, wsb_ref, o_ref):
    hsb = jnp.concatenate([hsb_ref[0, p] for p in range(SB_PAIRS)], axis=1)
    y = (jnp.dot(hml_ref[0], wml_ref[...], preferred_element_type=F32)
         + jnp.dot(hsb, wsb_ref[...], preferred_element_type=F32))
    r = lax.rsqrt(jnp.mean(y * y, axis=-1, keepdims=True) + EPS)
    o_ref[0] = x_ref[0] + gate_ref[0] * ((y * r) * gpost_ref[...])


def _outproj(hml, hsb, x, mod3, g_post, w_oml, w_osb):
    bsz, s, d = x.shape
    tm = min(OUT_TM, s)
    return pl.pallas_call(
        _outproj_kernel,
        grid=(bsz, s // tm),
        in_specs=[pl.BlockSpec((1, tm, ML_WIDTH), lambda b, i: (b, i, 0)),
                  pl.BlockSpec((1, SB_PAIRS, tm, LANES), lambda b, i: (b, 0, i, 0)),
                  pl.BlockSpec((1, tm, d), lambda b, i: (b, i, 0)),
                  pl.BlockSpec((1, 1, d), lambda b, i: (b, 0, 2)),
                  pl.BlockSpec((1, d), lambda b, i: (0, 0)),
                  pl.BlockSpec((ML_WIDTH, d), lambda b, i: (0, 0)),
                  pl.BlockSpec((SB_WIDTH, d), lambda b, i: (0, 0))],
        out_specs=pl.BlockSpec((1, tm, d), lambda b, i: (b, i, 0)),
        out_shape=jax.ShapeDtypeStruct((bsz, s, d), F32),
        compiler_params=pltpu.CompilerParams(
            dimension_semantics=("parallel", "parallel"), vmem_limit_bytes=VMEM_LIMIT),
        name="outproj",
    )(hml, hsb, x, mod3, g_post.reshape(1, d), w_oml, w_osb)


def kernel(x, c, w_ada, b_ada, g_pre, w_in, b_igate, b_fgate, conv_w, conv_b, g_ml_head, w_out, g_post):
    bsz, s, d = x.shape
    o_gate = 2 * ML_WIDTH + 3 * ML_WIDTH
    o_sb = o_gate + 2 * ML_HEADS
    w_ml = w_in[:, :o_gate].astype(BF16)
    w_g = jnp.pad(w_in[:, o_gate:o_sb], ((0, 0), (0, LANES - 2 * ML_HEADS))).astype(BF16)
    q_scale = jnp.where(jnp.arange(4 * SB_WIDTH) < SB_WIDTH, LOG2E * SB_HEAD_DIM ** -0.5, 1.0)
    w_sb = (w_in[:, o_sb:] * q_scale).astype(BF16)
    gbias = jnp.pad(jnp.concatenate([b_igate, b_fgate]), (0, LANES - 2 * ML_HEADS)).reshape(1, LANES)
    w_o = w_out.astype(BF16)

    mod3 = _adaln(c, w_ada, b_ada).reshape(bsz, 1, 3 * d)
    ml, sbq, sbk, sbv, sbz, gates = _inproj(x, mod3, g_pre, w_ml, w_sb, w_g)
    hml = _mlstm(ml, gates, conv_w, conv_b, gbias, g_ml_head)
    hsb = _stick_breaking(sbq, sbk, sbv, sbz)
    return _outproj(hml, hsb, x, mod3, g_post, w_o[:ML_WIDTH], w_o[ML_WIDTH:])
```

```python
import functools

import jax
import jax.numpy as jnp
from jax import lax
from jax.experimental import pallas as pl
from jax.experimental.pallas import tpu as pltpu

F32 = jnp.float32
BF16 = jnp.bfloat16
HIGHEST = lax.Precision.HIGHEST

EPS = 1e-6
ML_HEADS = 4
ML_HEAD_DIM = 128
ML_WIDTH = ML_HEADS * ML_HEAD_DIM
SB_HEADS = 8
SB_HEAD_DIM = 64
SB_WIDTH = SB_HEADS * SB_HEAD_DIM
SB_PAIRS = SB_HEADS // 2
CONV_K = 4
CHUNK = 64
LANES = 128
SUBLANES = 8

IN_TM = 512
ML_T = 256
SB_TQ = 128
SB_TK = 128
SB_ROWS = 256
OUT_TM = 512
LOG2E = 1.4426950408889634
SB_SKIP = 160.0
SB_DEAD = 1e30
SB_ZMAX = 126.0
VMEM_LIMIT = 56 * 1024 * 1024


def _sigmoid(x):
    return 0.5 + 0.5 * jnp.tanh(0.5 * x)


def _silu(x):
    hx = 0.5 * x
    return hx + hx * jnp.tanh(hx)


def _softplus(x):
    return jnp.maximum(x, 0.0) + jnp.log(1.0 + jnp.exp(-jnp.abs(x)))


def _adaln_kernel(c_ref, w_ref, b_ref, o_ref):
    c = c_ref[...]
    s = _silu(c)
    o_ref[...] = jnp.dot(s, w_ref[...], preferred_element_type=F32, precision=HIGHEST) + b_ref[...]


def _adaln(c, w_ada, b_ada):
    bsz, d = c.shape
    n = w_ada.shape[1]
    return pl.pallas_call(
        _adaln_kernel,
        grid=(n // d,),
        in_specs=[pl.BlockSpec((bsz, d), lambda j: (0, 0)),
                  pl.BlockSpec((d, d), lambda j: (0, j)),
                  pl.BlockSpec((1, d), lambda j: (0, j))],
        out_specs=pl.BlockSpec((bsz, d), lambda j: (0, j)),
        out_shape=jax.ShapeDtypeStruct((bsz, n), F32),
        name="adaln",
    )(c, w_ada, b_ada.reshape(1, n))


def _inproj_kernel(x_ref, shift_ref, scale_ref, gpre_ref, wml_ref, wvt_ref, wsb_ref, wg_ref,
                   ml_ref, vt_ref, sbq_ref, sbk_ref, sbv_ref, sbz_ref, gates_ref, h_scr):
    x = x_ref[0]
    r = lax.rsqrt(jnp.mean(x * x, axis=-1, keepdims=True) + EPS)
    a = gpre_ref[...] * (1.0 + scale_ref[0])
    h_scr[...] = ((x * r) * a + shift_ref[0]).astype(BF16)

    n_ml = wml_ref.shape[1]
    for n in range(n_ml // 512):
        cols = slice(n * 512, (n + 1) * 512)
        ml_ref[0, :, cols] = jnp.dot(h_scr[...], wml_ref[:, cols],
                                     preferred_element_type=F32).astype(BF16)
    tm = x.shape[0]
    vt = lax.dot_general(wvt_ref[...], h_scr[...], (((1,), (1,)), ((), ())),
                         preferred_element_type=F32).astype(BF16)
    for j in range(tm // CHUNK):
        vt_ref[0, j] = vt[:, j * CHUNK:(j + 1) * CHUNK]
    half = lax.broadcasted_iota(jnp.int32, (tm, LANES), 1) < SB_HEAD_DIM
    for n, dst in enumerate((sbq_ref, sbk_ref, sbv_ref, sbz_ref)):
        cols = slice(n * SB_WIDTH, (n + 1) * SB_WIDTH)
        res = jnp.dot(h_scr[...], wsb_ref[:, cols], preferred_element_type=F32).astype(BF16)
        for p in range(SB_PAIRS):
            blk = res[:, p * LANES:(p + 1) * LANES]
            if dst is sbk_ref or dst is sbv_ref:
                zero = jnp.zeros_like(blk)
                only_a = jnp.where(half, blk, zero)
                only_b = jnp.where(half, zero, blk)
                for j in range(tm // SB_TK):
                    rows = slice(j * SB_TK, (j + 1) * SB_TK)
                    dst[0, p, 2 * j * SB_TK:(2 * j + 1) * SB_TK, :] = only_a[rows]
                    dst[0, p, (2 * j + 1) * SB_TK:(2 * j + 2) * SB_TK, :] = only_b[rows]
            else:
                dst[0, p] = blk
    gates_ref[0] = jnp.dot(h_scr[...], wg_ref[...], preferred_element_type=F32)


def _inproj(x, mod3, g_pre, w_ml, w_vt, w_sb, w_g):
    bsz, s, d = x.shape
    tm = min(IN_TM, s)
    n_ml = w_ml.shape[1]
    vt_shape = jax.ShapeDtypeStruct((bsz, s // CHUNK, ML_WIDTH, CHUNK), BF16)
    vt_spec = pl.BlockSpec((1, tm // CHUNK, ML_WIDTH, CHUNK), lambda b, i: (b, i, 0, 0))
    sb_shape = jax.ShapeDtypeStruct((bsz, SB_PAIRS, s, LANES), BF16)
    sb_spec = pl.BlockSpec((1, SB_PAIRS, tm, LANES), lambda b, i: (b, 0, i, 0))
    kv_shape = jax.ShapeDtypeStruct((bsz, SB_PAIRS, 2 * s, LANES), BF16)
    kv_spec = pl.BlockSpec((1, SB_PAIRS, 2 * tm, LANES), lambda b, i: (b, 0, i, 0))
    return pl.pallas_call(
        _inproj_kernel,
        grid=(bsz, s // tm),
        in_specs=[pl.BlockSpec((1, tm, d), lambda b, i: (b, i, 0)),
                  pl.BlockSpec((1, 1, d), lambda b, i: (b, 0, 0)),
                  pl.BlockSpec((1, 1, d), lambda b, i: (b, 0, 1)),
                  pl.BlockSpec((1, d), lambda b, i: (0, 0)),
                  pl.BlockSpec((d, n_ml), lambda b, i: (0, 0)),
                  pl.BlockSpec((ML_WIDTH, d), lambda b, i: (0, 0)),
                  pl.BlockSpec((d, 4 * SB_WIDTH), lambda b, i: (0, 0)),
                  pl.BlockSpec((d, LANES), lambda b, i: (0, 0))],
        out_specs=[pl.BlockSpec((1, tm, n_ml), lambda b, i: (b, i, 0)),
                   vt_spec, sb_spec, kv_spec, kv_spec, sb_spec,
                   pl.BlockSpec((1, tm, LANES), lambda b, i: (b, i, 0))],
        out_shape=[jax.ShapeDtypeStruct((bsz, s, n_ml), BF16),
                   vt_shape, sb_shape, kv_shape, kv_shape, sb_shape,
                   jax.ShapeDtypeStruct((bsz, s, LANES), F32)],
        scratch_shapes=[pltpu.VMEM((tm, d), BF16)],
        compiler_params=pltpu.CompilerParams(
            dimension_semantics=("parallel", "parallel"), vmem_limit_bytes=VMEM_LIMIT),
        name="inproj",
    )(x, mod3, mod3, g_pre.reshape(1, d), w_ml, w_vt, w_sb, w_g)


def _mlstm_kernel(u_ref, v_ref, vt_ref, o_ref, z_ref, g_ref, convw_ref, convb_ref, gbias_ref,
                  ghead_ref, out_ref, ubuf, q_scr, k_scr, bc_scr, qkm_scr, cl_scr, nl_scr, cp_scr,
                  hg_scr, h2_scr, c_scr, n_scr, m_scr):
    t = u_ref.shape[1]
    n_chunks = t // CHUNK
    units = [(c, h) for c in range(n_chunks) for h in range(ML_HEADS)]
    j = pl.program_id(1)

    @pl.when(j == 0)
    def _():
        ubuf[0:SUBLANES, :] = jnp.zeros((SUBLANES, ubuf.shape[1]), F32)
        c_scr[...] = jnp.zeros_like(c_scr)
        n_scr[...] = jnp.zeros_like(n_scr)
        m_scr[...] = jnp.zeros_like(m_scr)

    @pl.when(j > 0)
    def _():
        ubuf[0:SUBLANES, :] = ubuf[t:t + SUBLANES, :]

    ubuf[SUBLANES:SUBLANES + t, :] = u_ref[0].astype(F32)

    conv = convb_ref[...]
    for i in range(CONV_K):
        off = SUBLANES - (CONV_K - 1) + i
        conv = conv + convw_ref[i:i + 1, :] * ubuf[off:off + t, :]
    qk = _silu(conv)
    q_scr[...] = qk[:, :ML_WIDTH].astype(BF16)
    k_scr[...] = qk[:, ML_WIDTH:] * (ML_HEAD_DIM ** -0.5)

    gates = g_ref[0] + gbias_ref[...]
    logf = -_softplus(-gates)
    row = lax.broadcasted_iota(jnp.int32, (t, t), 0)
    col = lax.broadcasted_iota(jnp.int32, (t, t), 1)
    tri = jnp.where((row >= col) & ((row // CHUNK) == (col // CHUNK)), 1.0, 0.0).astype(F32)
    bcum = jnp.dot(tri, logf, preferred_element_type=F32, precision=HIGHEST)
    bcum = pltpu.roll(bcum, LANES - ML_HEADS, 1)
    c_t = (gates.T - bcum.T)[0:SUBLANES, :]
    pos = lax.broadcasted_iota(jnp.int32, (SUBLANES, t), 1) & (CHUNK - 1)
    cm_t = c_t
    step = 1
    while step < CHUNK:
        cm_t = jnp.maximum(cm_t, jnp.where(pos >= step, pltpu.roll(cm_t, step, 1), -jnp.inf))
        step *= 2
    cm = jnp.concatenate([cm_t, jnp.zeros((LANES - SUBLANES, t), F32)], axis=0).T

    m_run = m_scr[0:1, :]
    s_old, s_loc, stats = [], [], []
    for c in range(n_chunks):
        rows = slice(c * CHUNK, (c + 1) * CHUNK)
        b_c = bcum[rows]
        a_c = b_c[CHUNK - 1:CHUNK, :]
        g_c = a_c - b_c + gates[rows]
        m_loc = jnp.max(g_c, axis=0, keepdims=True)
        mm = jnp.maximum(m_run, cm[rows])
        stats.append((mm, jnp.exp(m_run - mm), jnp.exp(-(b_c + mm)), jnp.exp(g_c - m_loc)))
        m_new = jnp.maximum(a_c + m_run, m_loc)
        s_old.append(jnp.exp(a_c + m_run - m_new))
        s_loc.append(jnp.exp(m_loc - m_new))
        m_run = m_new
    m_scr[0:1, :] = m_run

    for u, (c, h) in enumerate(units):
        for i, stat in enumerate(stats[c]):
            bc_scr[u, i] = jnp.broadcast_to(stat[:, h:h + 1], (CHUNK, LANES))

    causal = (lax.broadcasted_iota(jnp.int32, (CHUNK, CHUNK), 0)
              >= lax.broadcasted_iota(jnp.int32, (CHUNK, CHUNK), 1))
    ones_rows = jnp.ones((2 * SUBLANES, CHUNK), BF16)

    for u, (c, h) in enumerate(units):
        rows = slice(c * CHUNK, (c + 1) * CHUNK)
        lanes = slice(h * ML_HEAD_DIM, (h + 1) * ML_HEAD_DIM)
        k_c = k_scr[rows, lanes]
        s_qk = lax.dot_general(q_scr[rows, lanes], k_c.astype(BF16), (((1,), (1,)), ((), ())),
                               preferred_element_type=F32)
        dw = jnp.where(causal, jnp.exp(c_t[h:h + 1, rows] - bc_scr[u, 0][:, :CHUNK]), 0.0)
        qkm_scr[u] = (s_qk * dw).astype(BF16)
        wk = (bc_scr[u, 3] * k_c).astype(BF16)
        lhs = jnp.concatenate([vt_ref[0, c, lanes, :], ones_rows], axis=0)
        res = jnp.dot(lhs, wk, preferred_element_type=F32)
        cl_scr[u] = res[:ML_HEAD_DIM]
        nl_scr[u, 0:1, :] = res[ML_HEAD_DIM:ML_HEAD_DIM + 1]

    for h in range(ML_HEADS):
        c_run = c_scr[h]
        n_run = n_scr[h:h + 1, :]
        for c in range(n_chunks):
            u = c * ML_HEADS + h
            cp_scr[u, 0:ML_HEAD_DIM, :] = c_run.astype(BF16)
            cp_scr[u, ML_HEAD_DIM:2 * ML_HEAD_DIM, :] = jnp.broadcast_to(
                n_run, (ML_HEAD_DIM, ML_HEAD_DIM)).astype(BF16)
            so = s_old[c][:, h:h + 1]
            sl = s_loc[c][:, h:h + 1]
            c_run = so * c_run + sl * cl_scr[u]
            n_run = so * n_run + sl * nl_scr[u, 0:1, :]
        c_scr[h] = c_run
        n_scr[h:h + 1, :] = n_run

    ones_cols = jnp.ones((CHUNK, ML_HEAD_DIM), BF16)
    for u, (c, h) in enumerate(units):
        rows = slice(c * CHUNK, (c + 1) * CHUNK)
        lanes = slice(h * ML_HEAD_DIM, (h + 1) * ML_HEAD_DIM)
        qc = lax.dot_general(q_scr[rows, lanes], cp_scr[u], (((1,), (1,)), ((), ())),
                             preferred_element_type=F32)
        kv = jnp.dot(qkm_scr[u], jnp.concatenate([v_ref[0, rows, lanes], ones_cols], axis=1),
                     preferred_element_type=F32)
        inter_w = bc_scr[u, 1]
        den = inter_w * qc[:, ML_HEAD_DIM:] + kv[:, ML_HEAD_DIM:]
        num = inter_w * qc[:, :ML_HEAD_DIM] + kv[:, :ML_HEAD_DIM]
        hh = num * (1.0 / jnp.maximum(jnp.abs(den), bc_scr[u, 2]))
        hg = _sigmoid(o_ref[0, rows, lanes].astype(F32)) * hh
        hg_scr[u] = hg
        h2_scr[u] = (hg * hg).astype(BF16)

    ones_sq = jnp.ones((ML_HEAD_DIM, ML_HEAD_DIM), BF16)
    for u, (c, h) in enumerate(units):
        rows = slice(c * CHUNK, (c + 1) * CHUNK)
        lanes = slice(h * ML_HEAD_DIM, (h + 1) * ML_HEAD_DIM)
        ss = jnp.dot(h2_scr[u], ones_sq, preferred_element_type=F32)
        hn = hg_scr[u] * lax.rsqrt(ss * (1.0 / ML_HEAD_DIM) + EPS) * ghead_ref[:, lanes]
        out_ref[0, rows, lanes] = (hn * _silu(z_ref[0, rows, lanes].astype(F32))).astype(BF16)


def _mlstm(ml, vt, gates, conv_w, conv_b, gbias, g_head):
    bsz, s, _ = ml.shape
    t = min(ML_T, s)
    w2 = 2 * ML_WIDTH
    n_units = (t // CHUNK) * ML_HEADS
    return pl.pallas_call(
        _mlstm_kernel,
        grid=(bsz, s // t),
        in_specs=[pl.BlockSpec((1, t, w2), lambda b, j: (b, j, 0)),
                  pl.BlockSpec((1, t, ML_WIDTH), lambda b, j: (b, j, 2)),
                  pl.BlockSpec((1, t // CHUNK, ML_WIDTH, CHUNK), lambda b, j: (b, j, 0, 0)),
                  pl.BlockSpec((1, t, ML_WIDTH), lambda b, j: (b, j, 3)),
                  pl.BlockSpec((1, t, ML_WIDTH), lambda b, j: (b, j, 4)),
                  pl.BlockSpec((1, t, LANES), lambda b, j: (b, j, 0)),
                  pl.BlockSpec((CONV_K, w2), lambda b, j: (0, 0)),
                  pl.BlockSpec((1, w2), lambda b, j: (0, 0)),
                  pl.BlockSpec((1, LANES), lambda b, j: (0, 0)),
                  pl.BlockSpec((1, ML_WIDTH), lambda b, j: (0, 0))],
        out_specs=pl.BlockSpec((1, t, ML_WIDTH), lambda b, j: (b, j, 0)),
        out_shape=jax.ShapeDtypeStruct((bsz, s, ML_WIDTH), BF16),
        scratch_shapes=[pltpu.VMEM((t + 2 * SUBLANES, w2), F32),
                        pltpu.VMEM((t, ML_WIDTH), BF16),
                        pltpu.VMEM((t, ML_WIDTH), F32),
                        pltpu.VMEM((n_units, 4, CHUNK, LANES), F32),
                        pltpu.VMEM((n_units, CHUNK, CHUNK), BF16),
                        pltpu.VMEM((n_units, ML_HEAD_DIM, ML_HEAD_DIM), F32),
                        pltpu.VMEM((n_units, SUBLANES, ML_HEAD_DIM), F32),
                        pltpu.VMEM((n_units, 2 * ML_HEAD_DIM, ML_HEAD_DIM), BF16),
                        pltpu.VMEM((n_units, CHUNK, ML_HEAD_DIM), F32),
                        pltpu.VMEM((n_units, CHUNK, ML_HEAD_DIM), BF16),
                        pltpu.VMEM((ML_HEADS, ML_HEAD_DIM, ML_HEAD_DIM), F32),
                        pltpu.VMEM((SUBLANES, ML_HEAD_DIM), F32),
                        pltpu.VMEM((SUBLANES, LANES), F32)],
        compiler_params=pltpu.CompilerParams(
            dimension_semantics=("parallel", "arbitrary"), vmem_limit_bytes=VMEM_LIMIT),
        name="mlstm",
    )(ml, ml, vt, ml, ml, gates, conv_w, conv_b.reshape(1, w2), gbias, g_head.reshape(1, ML_WIDTH))


def _sb_sweep(d, diag, qt0, chains, q_ref, k_ref, v_ref, cum_ref, z_scr, n_scr, a_scr, s_scr,
              acc_scr, r_scr):
    if diag:
        strict = ((lax.broadcasted_iota(jnp.int32, (SB_TQ, 2 * SB_TK), 1) & (SB_TK - 1))
                  < lax.broadcasted_iota(jnp.int32, (SB_TQ, 2 * SB_TK), 0))
    lane_a = lax.broadcasted_iota(jnp.int32, (SB_TQ, LANES), 1) < SB_HEAD_DIM

    def key_rows(g):
        return pl.ds(jnp.maximum(qt0 + g - d, 0) * (2 * SB_TK), 2 * SB_TK)

    for c, (g, p) in enumerate(chains):
        q2 = q_ref[0, p, g * SB_TQ:(g + 1) * SB_TQ, :]
        z = lax.dot_general(q2, k_ref[0, p, key_rows(g), :], (((1,), (1,)), ((), ())),
                            preferred_element_type=F32)
        nlk = jnp.log(1.0 + jnp.exp2(jnp.minimum(z, SB_ZMAX))) * LOG2E
        if diag:
            nlk = jnp.where(strict, nlk, 0.0)
        z_scr[c] = z
        n_scr[c] = nlk.astype(BF16)

    for c, (g, p) in enumerate(chains):
        m = jnp.dot(n_scr[c], cum_ref[...], preferred_element_type=F32)
        a = jnp.exp2(z_scr[c] - m)
        if diag:
            a = jnp.where(strict, a, 0.0)
        a_scr[c] = a.astype(BF16)
        s_scr[c] = jnp.where(lane_a, m[:, 0:1], m[:, SB_TK:SB_TK + 1])

    rmin = None
    for c, (g, p) in enumerate(chains):
        pv = jnp.dot(a_scr[c], v_ref[0, p, key_rows(g), :], preferred_element_type=F32)
        if diag:
            acc_scr[c] = pv
            r_new = s_scr[c]
        else:
            r = jnp.where(qt0 + g - d >= 0, r_scr[c], SB_DEAD)
            acc_scr[c] += jnp.exp2(-r) * pv
            r_new = r + s_scr[c]
        r_scr[c] = r_new
        rmin = r_new if rmin is None else jnp.minimum(rmin, r_new)
    return jnp.min(rmin)


def _sb_kernel(q_ref, k_ref, v_ref, z_ref, cum_ref, o_ref, z_scr, n_scr, a_scr, s_scr, acc_scr, r_scr):
    n_sub = q_ref.shape[2] // SB_TQ
    qt0 = pl.program_id(1) * n_sub
    chains = [(g, p) for g in range(n_sub) for p in range(SB_PAIRS)]
    sweep = functools.partial(_sb_sweep, qt0=qt0, chains=chains, q_ref=q_ref, k_ref=k_ref,
                              v_ref=v_ref, cum_ref=cum_ref, z_scr=z_scr, n_scr=n_scr, a_scr=a_scr,
                              s_scr=s_scr, acc_scr=acc_scr, r_scr=r_scr)

    def cond(carry):
        d, rm = carry
        return jnp.logical_and(qt0 + (n_sub - 1) - d >= 0, rm < SB_SKIP)

    def body(carry):
        d, _ = carry
        return d + 1, sweep(d, False)

    lax.while_loop(cond, body, (jnp.int32(1), sweep(0, True)))

    for c, (g, p) in enumerate(chains):
        zc = z_ref[0, p, g * SB_TQ:(g + 1) * SB_TQ, :].astype(F32)
        o_ref[0, p, g * SB_TQ:(g + 1) * SB_TQ, :] = (acc_scr[c] * _silu(zc)).astype(BF16)


def _stick_breaking(sbq, sbk, sbv, sbz):
    bsz, pairs, s, _ = sbq.shape
    rows = min(SB_ROWS, s)
    j = lax.broadcasted_iota(jnp.int32, (2 * SB_TK, 2 * SB_TK), 0)
    c = lax.broadcasted_iota(jnp.int32, (2 * SB_TK, 2 * SB_TK), 1)
    cum = jnp.where((j >= c) & ((j // SB_TK) == (c // SB_TK)), 1.0, 0.0).astype(BF16)
    q_spec = pl.BlockSpec((1, pairs, rows, LANES), lambda b, i: (b, 0, i, 0))
    kv_spec = pl.BlockSpec((1, pairs, 2 * s, LANES), lambda b, i: (b, 0, 0, 0))
    n_chain = (rows // SB_TQ) * pairs
    return pl.pallas_call(
        _sb_kernel,
        grid=(bsz, s // rows),
        in_specs=[q_spec, kv_spec, kv_spec, q_spec,
                  pl.BlockSpec((2 * SB_TK, 2 * SB_TK), lambda b, i: (0, 0))],
        out_specs=q_spec,
        out_shape=jax.ShapeDtypeStruct((bsz, pairs, s, LANES), BF16),
        scratch_shapes=[pltpu.VMEM((n_chain, SB_TQ, 2 * SB_TK), F32),
                        pltpu.VMEM((n_chain, SB_TQ, 2 * SB_TK), BF16),
                        pltpu.VMEM((n_chain, SB_TQ, 2 * SB_TK), BF16),
                        pltpu.VMEM((n_chain, SB_TQ, LANES), F32),
                        pltpu.VMEM((n_chain, SB_TQ, LANES), F32),
                        pltpu.VMEM((n_chain, SB_TQ, LANES), F32)],
        compiler_params=pltpu.CompilerParams(
            dimension_semantics=("parallel", "arbitrary"), vmem_limit_bytes=VMEM_LIMIT),
        name="stickbreak",
    )(sbq, sbk, sbv, sbz, cum)


def _outproj_kernel(hml_ref, hsb_ref, x_ref, gate_ref, gpost_ref, wml_ref, wsb_ref, o_ref):
    hsb = jnp.concatenate([hsb_ref[0, p] for p in range(SB_PAIRS)], axis=1)
    y = (jnp.dot(hml_ref[0], wml_ref[...], preferred_element_type=F32)
         + jnp.dot(hsb, wsb_ref[...], preferred_element_type=F32))
    r = lax.rsqrt(jnp.mean(y * y, axis=-1, keepdims=True) + EPS)
    o_ref[0] = x_ref[0] + gate_ref[0] * ((y * r) * gpost_ref[...])


def _outproj(hml, hsb, x, mod3, g_post, w_oml, w_osb):
    bsz, s, d = x.shape
    tm = min(OUT_TM, s)
    return pl.pallas_call(
        _outproj_kernel,
        grid=(bsz, s // tm),
        in_specs=[pl.BlockSpec((1, tm, ML_WIDTH), lambda b, i: (b, i, 0)),
                  pl.BlockSpec((1, SB_PAIRS, tm, LANES), lambda b, i: (b, 0, i, 0)),
                  pl.BlockSpec((1, tm, d), lambda b, i: (b, i, 0)),
                  pl.BlockSpec((1, 1, d), lambda b, i: (b, 0, 2)),
                  pl.BlockSpec((1, d), lambda b, i: (0, 0)),
                  pl.BlockSpec((ML_WIDTH, d), lambda b, i: (0, 0)),
                  pl.BlockSpec((SB_WIDTH, d), lambda b, i: (0, 0))],
        out_specs=pl.BlockSpec((1, tm, d), lambda b, i: (b, i, 0)),
        out_shape=jax.ShapeDtypeStruct((bsz, s, d), F32),
        compiler_params=pltpu.CompilerParams(
            dimension_semantics=("parallel", "parallel"), vmem_limit_bytes=VMEM_LIMIT),
        name="outproj",
    )(hml, hsb, x, mod3, g_post.reshape(1, d), w_oml, w_osb)


def kernel(x, c, w_ada, b_ada, g_pre, w_in, b_igate, b_fgate, conv_w, conv_b, g_ml_head, w_out, g_post):
    bsz, s, d = x.shape
    o_gate = 2 * ML_WIDTH + 3 * ML_WIDTH
    o_sb = o_gate + 2 * ML_HEADS
    w_ml = w_in[:, :o_gate].astype(BF16)
    w_g = jnp.pad(w_in[:, o_gate:o_sb], ((0, 0), (0, LANES - 2 * ML_HEADS))).astype(BF16)
    q_scale = jnp.where(jnp.arange(4 * SB_WIDTH) < SB_WIDTH, LOG2E * SB_HEAD_DIM ** -0.5, 1.0)
    w_sb = (w_in[:, o_sb:] * q_scale).astype(BF16)
    gbias = jnp.pad(jnp.concatenate([b_igate, b_fgate]), (0, LANES - 2 * ML_HEADS)).reshape(1, LANES)
    w_o = w_out.astype(BF16)

    mod3 = _adaln(c, w_ada, b_ada).reshape(bsz, 1, 3 * d)
    w_vt = w_in[:, 2 * ML_WIDTH:3 * ML_WIDTH].T.astype(BF16)
    ml, vt, sbq, sbk, sbv, sbz, gates = _inproj(x, mod3, g_pre, w_ml, w_vt, w_sb, w_g)
    hml = _mlstm(ml, vt, gates, conv_w, conv_b, gbias, g_ml_head)
    hsb = _stick_breaking(sbq, sbk, sbv, sbz)
    return _outproj(hml, hsb, x, mod3, g_post, w_o[:ML_WIDTH], w_o[ML_WIDTH:])
```

```python
import functools

import jax
import jax.numpy as jnp
from jax import lax
from jax.experimental import pallas as pl
from jax.experimental.pallas import tpu as pltpu

F32 = jnp.float32
BF16 = jnp.bfloat16
HIGHEST = lax.Precision.HIGHEST

EPS = 1e-6
ML_HEADS = 4
ML_HEAD_DIM = 128
ML_WIDTH = ML_HEADS * ML_HEAD_DIM
SB_HEADS = 8
SB_HEAD_DIM = 64
SB_WIDTH = SB_HEADS * SB_HEAD_DIM
SB_PAIRS = SB_HEADS // 2
CONV_K = 4
CHUNK = 64
LANES = 128
SUBLANES = 8

IN_TM = 512
ML_T = 256
SB_TQ = 128
SB_TK = 128
SB_ROWS = 256
OUT_TM = 512
LOG2E = 1.4426950408889634
SB_SKIP = 160.0
SB_DEAD = 1e30
SB_MIN_SWEEPS = 3
SB_ZMAX = 126.0
VMEM_LIMIT = 56 * 1024 * 1024


def _sigmoid(x):
    return 0.5 + 0.5 * jnp.tanh(0.5 * x)


def _silu(x):
    hx = 0.5 * x
    return hx + hx * jnp.tanh(hx)


def _softplus(x):
    return jnp.maximum(x, 0.0) + jnp.log(1.0 + jnp.exp(-jnp.abs(x)))


def _adaln_kernel(c_ref, w_ref, b_ref, o_ref):
    c = c_ref[...]
    s = _silu(c)
    o_ref[...] = jnp.dot(s, w_ref[...], preferred_element_type=F32, precision=HIGHEST) + b_ref[...]


def _adaln(c, w_ada, b_ada):
    bsz, d = c.shape
    n = w_ada.shape[1]
    return pl.pallas_call(
        _adaln_kernel,
        grid=(n // d,),
        in_specs=[pl.BlockSpec((bsz, d), lambda j: (0, 0)),
                  pl.BlockSpec((d, d), lambda j: (0, j)),
                  pl.BlockSpec((1, d), lambda j: (0, j))],
        out_specs=pl.BlockSpec((bsz, d), lambda j: (0, j)),
        out_shape=jax.ShapeDtypeStruct((bsz, n), F32),
        name="adaln",
    )(c, w_ada, b_ada.reshape(1, n))


def _inproj_kernel(x_ref, shift_ref, scale_ref, gpre_ref, wml_ref, wvt_ref, wsb_ref, wg_ref,
                   ml_ref, vt_ref, sbq_ref, sbk_ref, sbv_ref, sbz_ref, gates_ref, h_scr):
    x = x_ref[0]
    r = lax.rsqrt(jnp.mean(x * x, axis=-1, keepdims=True) + EPS)
    a = gpre_ref[...] * (1.0 + scale_ref[0])
    h_scr[...] = ((x * r) * a + shift_ref[0]).astype(BF16)

    n_ml = wml_ref.shape[1]
    for n in range(n_ml // 512):
        cols = slice(n * 512, (n + 1) * 512)
        ml_ref[0, :, cols] = jnp.dot(h_scr[...], wml_ref[:, cols],
                                     preferred_element_type=F32).astype(BF16)
    tm = x.shape[0]
    vt = lax.dot_general(wvt_ref[...], h_scr[...], (((1,), (1,)), ((), ())),
                         preferred_element_type=F32).astype(BF16)
    for j in range(tm // CHUNK):
        vt_ref[0, j] = vt[:, j * CHUNK:(j + 1) * CHUNK]
    half = lax.broadcasted_iota(jnp.int32, (tm, LANES), 1) < SB_HEAD_DIM
    for n, dst in enumerate((sbq_ref, sbk_ref, sbv_ref, sbz_ref)):
        cols = slice(n * SB_WIDTH, (n + 1) * SB_WIDTH)
        res = jnp.dot(h_scr[...], wsb_ref[:, cols], preferred_element_type=F32).astype(BF16)
        for p in range(SB_PAIRS):
            blk = res[:, p * LANES:(p + 1) * LANES]
            if dst is sbk_ref or dst is sbv_ref:
                zero = jnp.zeros_like(blk)
                only_a = jnp.where(half, blk, zero)
                only_b = jnp.where(half, zero, blk)
                for j in range(tm // SB_TK):
                    rows = slice(j * SB_TK, (j + 1) * SB_TK)
                    dst[0, p, 2 * j * SB_TK:(2 * j + 1) * SB_TK, :] = only_a[rows]
                    dst[0, p, (2 * j + 1) * SB_TK:(2 * j + 2) * SB_TK, :] = only_b[rows]
            else:
                dst[0, p] = blk
    gates_ref[0] = jnp.dot(h_scr[...], wg_ref[...], preferred_element_type=F32)


def _inproj(x, mod3, g_pre, w_ml, w_vt, w_sb, w_g):
    bsz, s, d = x.shape
    tm = min(IN_TM, s)
    n_ml = w_ml.shape[1]
    vt_shape = jax.ShapeDtypeStruct((bsz, s // CHUNK, ML_WIDTH, CHUNK), BF16)
    vt_spec = pl.BlockSpec((1, tm // CHUNK, ML_WIDTH, CHUNK), lambda b, i: (b, i, 0, 0))
    sb_shape = jax.ShapeDtypeStruct((bsz, SB_PAIRS, s, LANES), BF16)
    sb_spec = pl.BlockSpec((1, SB_PAIRS, tm, LANES), lambda b, i: (b, 0, i, 0))
    kv_shape = jax.ShapeDtypeStruct((bsz, SB_PAIRS, 2 * s, LANES), BF16)
    kv_spec = pl.BlockSpec((1, SB_PAIRS, 2 * tm, LANES), lambda b, i: (b, 0, i, 0))
    return pl.pallas_call(
        _inproj_kernel,
        grid=(bsz, s // tm),
        in_specs=[pl.BlockSpec((1, tm, d), lambda b, i: (b, i, 0)),
                  pl.BlockSpec((1, 1, d), lambda b, i: (b, 0, 0)),
                  pl.BlockSpec((1, 1, d), lambda b, i: (b, 0, 1)),
                  pl.BlockSpec((1, d), lambda b, i: (0, 0)),
                  pl.BlockSpec((d, n_ml), lambda b, i: (0, 0)),
                  pl.BlockSpec((ML_WIDTH, d), lambda b, i: (0, 0)),
                  pl.BlockSpec((d, 4 * SB_WIDTH), lambda b, i: (0, 0)),
                  pl.BlockSpec((d, LANES), lambda b, i: (0, 0))],
        out_specs=[pl.BlockSpec((1, tm, n_ml), lambda b, i: (b, i, 0)),
                   vt_spec, sb_spec, kv_spec, kv_spec, sb_spec,
                   pl.BlockSpec((1, tm, LANES), lambda b, i: (b, i, 0))],
        out_shape=[jax.ShapeDtypeStruct((bsz, s, n_ml), BF16),
                   vt_shape, sb_shape, kv_shape, kv_shape, sb_shape,
                   jax.ShapeDtypeStruct((bsz, s, LANES), F32)],
        scratch_shapes=[pltpu.VMEM((tm, d), BF16)],
        compiler_params=pltpu.CompilerParams(
            dimension_semantics=("parallel", "parallel"), vmem_limit_bytes=VMEM_LIMIT),
        name="inproj",
    )(x, mod3, mod3, g_pre.reshape(1, d), w_ml, w_vt, w_sb, w_g)


def _mlstm_kernel(u_ref, v_ref, vt_ref, o_ref, z_ref, g_ref, convw_ref, convb_ref, gbias_ref,
                  ghead_ref, out_ref, ubuf, q_scr, k_scr, bc_scr, qkm_scr, cl_scr, nl_scr, cp_scr,
                  hg_scr, h2_scr, c_scr, n_scr, m_scr):
    t = u_ref.shape[1]
    n_chunks = t // CHUNK
    units = [(c, h) for c in range(n_chunks) for h in range(ML_HEADS)]
    j = pl.program_id(1)

    @pl.when(j == 0)
    def _():
        ubuf[0:SUBLANES, :] = jnp.zeros((SUBLANES, ubuf.shape[1]), F32)
        c_scr[...] = jnp.zeros_like(c_scr)
        n_scr[...] = jnp.zeros_like(n_scr)
        m_scr[...] = jnp.zeros_like(m_scr)

    @pl.when(j > 0)
    def _():
        ubuf[0:SUBLANES, :] = ubuf[t:t + SUBLANES, :]

    ubuf[SUBLANES:SUBLANES + t, :] = u_ref[0].astype(F32)

    conv = convb_ref[...]
    for i in range(CONV_K):
        off = SUBLANES - (CONV_K - 1) + i
        conv = conv + convw_ref[i:i + 1, :] * ubuf[off:off + t, :]
    qk = _silu(conv)
    q_scr[...] = qk[:, :ML_WIDTH].astype(BF16)
    k_scr[...] = qk[:, ML_WIDTH:] * (ML_HEAD_DIM ** -0.5)

    gates = g_ref[0] + gbias_ref[...]
    pos = lax.broadcasted_iota(jnp.int32, (t, LANES), 0) & (CHUNK - 1)
    bcum = -_softplus(-gates)
    step = 1
    while step < CHUNK:
        bcum = bcum + jnp.where(pos >= step, pltpu.roll(bcum, step, 0), 0.0)
        step *= 2
    bcum = pltpu.roll(bcum, LANES - ML_HEADS, 1)
    cdiff = gates - bcum
    cm = cdiff
    step = 1
    while step < CHUNK:
        cm = jnp.maximum(cm, jnp.where(pos >= step, pltpu.roll(cm, step, 0), -jnp.inf))
        step *= 2
    c_t = cdiff.T[0:SUBLANES, :]

    m_run = m_scr[0:1, :]
    s_old, s_loc, stats = [], [], []
    for c in range(n_chunks):
        rows = slice(c * CHUNK, (c + 1) * CHUNK)
        b_c = bcum[rows]
        a_c = b_c[CHUNK - 1:CHUNK, :]
        g_c = a_c - b_c + gates[rows]
        m_loc = jnp.max(g_c, axis=0, keepdims=True)
        mm = jnp.maximum(m_run, cm[rows])
        stats.append((mm, jnp.exp(m_run - mm), jnp.exp(-(b_c + mm)), jnp.exp(g_c - m_loc)))
        m_new = jnp.maximum(a_c + m_run, m_loc)
        s_old.append(jnp.exp(a_c + m_run - m_new))
        s_loc.append(jnp.exp(m_loc - m_new))
        m_run = m_new
    m_scr[0:1, :] = m_run

    for u, (c, h) in enumerate(units):
        for i, stat in enumerate(stats[c]):
            bc_scr[u, i] = jnp.broadcast_to(stat[:, h:h + 1], (CHUNK, LANES))

    causal = (lax.broadcasted_iota(jnp.int32, (CHUNK, CHUNK), 0)
              >= lax.broadcasted_iota(jnp.int32, (CHUNK, CHUNK), 1))
    ones_rows = jnp.ones((2 * SUBLANES, CHUNK), BF16)

    for u, (c, h) in enumerate(units):
        rows = slice(c * CHUNK, (c + 1) * CHUNK)
        lanes = slice(h * ML_HEAD_DIM, (h + 1) * ML_HEAD_DIM)
        k_c = k_scr[rows, lanes]
        s_qk = lax.dot_general(q_scr[rows, lanes], k_c.astype(BF16), (((1,), (1,)), ((), ())),
                               preferred_element_type=F32)
        dw = jnp.where(causal, jnp.exp(c_t[h:h + 1, rows] - bc_scr[u, 0][:, :CHUNK]), 0.0)
        qkm_scr[u] = (s_qk * dw).astype(BF16)
        wk = (bc_scr[u, 3] * k_c).astype(BF16)
        lhs = jnp.concatenate([vt_ref[0, c, lanes, :], ones_rows], axis=0)
        res = jnp.dot(lhs, wk, preferred_element_type=F32)
        cl_scr[u] = res[:ML_HEAD_DIM]
        nl_scr[u, 0:1, :] = res[ML_HEAD_DIM:ML_HEAD_DIM + 1]

    for h in range(ML_HEADS):
        c_run = c_scr[h]
        n_run = n_scr[h:h + 1, :]
        for c in range(n_chunks):
            u = c * ML_HEADS + h
            cp_scr[u, 0:ML_HEAD_DIM, :] = c_run.astype(BF16)
            cp_scr[u, ML_HEAD_DIM:2 * ML_HEAD_DIM, :] = jnp.broadcast_to(
                n_run, (ML_HEAD_DIM, ML_HEAD_DIM)).astype(BF16)
            so = s_old[c][:, h:h + 1]
            sl = s_loc[c][:, h:h + 1]
            c_run = so * c_run + sl * cl_scr[u]
            n_run = so * n_run + sl * nl_scr[u, 0:1, :]
        c_scr[h] = c_run
        n_scr[h:h + 1, :] = n_run

    ones_cols = jnp.ones((CHUNK, ML_HEAD_DIM), BF16)
    for u, (c, h) in enumerate(units):
        rows = slice(c * CHUNK, (c + 1) * CHUNK)
        lanes = slice(h * ML_HEAD_DIM, (h + 1) * ML_HEAD_DIM)
        qc = lax.dot_general(q_scr[rows, lanes], cp_scr[u], (((1,), (1,)), ((), ())),
                             preferred_element_type=F32)
        kv = jnp.dot(qkm_scr[u], jnp.concatenate([v_ref[0, rows, lanes], ones_cols], axis=1),
                     preferred_element_type=F32)
        inter_w = bc_scr[u, 1]
        den = inter_w * qc[:, ML_HEAD_DIM:] + kv[:, ML_HEAD_DIM:]
        num = inter_w * qc[:, :ML_HEAD_DIM] + kv[:, :ML_HEAD_DIM]
        hh = num * (1.0 / jnp.maximum(jnp.abs(den), bc_scr[u, 2]))
        hg = _sigmoid(o_ref[0, rows, lanes].astype(F32)) * hh
        hg_scr[u] = hg
        h2_scr[u] = (hg * hg).astype(BF16)

    ones_sq = jnp.ones((ML_HEAD_DIM, ML_HEAD_DIM), BF16)
    for u, (c, h) in enumerate(units):
        rows = slice(c * CHUNK, (c + 1) * CHUNK)
        lanes = slice(h * ML_HEAD_DIM, (h + 1) * ML_HEAD_DIM)
        ss = jnp.dot(h2_scr[u], ones_sq, preferred_element_type=F32)
        hn = hg_scr[u] * lax.rsqrt(ss * (1.0 / ML_HEAD_DIM) + EPS) * ghead_ref[:, lanes]
        out_ref[0, rows, lanes] = (hn * _silu(z_ref[0, rows, lanes].astype(F32))).astype(BF16)


def _mlstm(ml, vt, gates, conv_w, conv_b, gbias, g_head):
    bsz, s, _ = ml.shape
    t = min(ML_T, s)
    w2 = 2 * ML_WIDTH
    n_units = (t // CHUNK) * ML_HEADS
    return pl.pallas_call(
        _mlstm_kernel,
        grid=(bsz, s // t),
        in_specs=[pl.BlockSpec((1, t, w2), lambda b, j: (b, j, 0)),
                  pl.BlockSpec((1, t, ML_WIDTH), lambda b, j: (b, j, 2)),
                  pl.BlockSpec((1, t // CHUNK, ML_WIDTH, CHUNK), lambda b, j: (b, j, 0, 0)),
                  pl.BlockSpec((1, t, ML_WIDTH), lambda b, j: (b, j, 3)),
                  pl.BlockSpec((1, t, ML_WIDTH), lambda b, j: (b, j, 4)),
                  pl.BlockSpec((1, t, LANES), lambda b, j: (b, j, 0)),
                  pl.BlockSpec((CONV_K, w2), lambda b, j: (0, 0)),
                  pl.BlockSpec((1, w2), lambda b, j: (0, 0)),
                  pl.BlockSpec((1, LANES), lambda b, j: (0, 0)),
                  pl.BlockSpec((1, ML_WIDTH), lambda b, j: (0, 0))],
        out_specs=pl.BlockSpec((1, t, ML_WIDTH), lambda b, j: (b, j, 0)),
        out_shape=jax.ShapeDtypeStruct((bsz, s, ML_WIDTH), BF16),
        scratch_shapes=[pltpu.VMEM((t + 2 * SUBLANES, w2), F32),
                        pltpu.VMEM((t, ML_WIDTH), BF16),
                        pltpu.VMEM((t, ML_WIDTH), F32),
                        pltpu.VMEM((n_units, 4, CHUNK, LANES), F32),
                        pltpu.VMEM((n_units, CHUNK, CHUNK), BF16),
                        pltpu.VMEM((n_units, ML_HEAD_DIM, ML_HEAD_DIM), F32),
                        pltpu.VMEM((n_units, SUBLANES, ML_HEAD_DIM), F32),
                        pltpu.VMEM((n_units, 2 * ML_HEAD_DIM, ML_HEAD_DIM), BF16),
                        pltpu.VMEM((n_units, CHUNK, ML_HEAD_DIM), F32),
                        pltpu.VMEM((n_units, CHUNK, ML_HEAD_DIM), BF16),
                        pltpu.VMEM((ML_HEADS, ML_HEAD_DIM, ML_HEAD_DIM), F32),
                        pltpu.VMEM((SUBLANES, ML_HEAD_DIM), F32),
                        pltpu.VMEM((SUBLANES, LANES), F32)],
        compiler_params=pltpu.CompilerParams(
            dimension_semantics=("parallel", "arbitrary"), vmem_limit_bytes=VMEM_LIMIT),
        name="mlstm",
    )(ml, ml, vt, ml, ml, gates, conv_w, conv_b.reshape(1, w2), gbias, g_head.reshape(1, ML_WIDTH))


def _sb_sweeps(ds, first, qt0, chains, q_ref, k_ref, v_ref, cum_ref, z_scr, n_scr, a_scr, s_scr,
               acc_scr, r_scr):
    n_chain = len(chains)
    items = [(i * n_chain + c, d, first and i == 0, g, p)
             for i, d in enumerate(ds) for c, (g, p) in enumerate(chains)]
    if first:
        strict = ((lax.broadcasted_iota(jnp.int32, (SB_TQ, 2 * SB_TK), 1) & (SB_TK - 1))
                  < lax.broadcasted_iota(jnp.int32, (SB_TQ, 2 * SB_TK), 0))
    lane_a = lax.broadcasted_iota(jnp.int32, (SB_TQ, LANES), 1) < SB_HEAD_DIM

    def key_rows(g, d):
        return pl.ds(jnp.maximum(qt0 + g - d, 0) * (2 * SB_TK), 2 * SB_TK)

    for slot, d, diag, g, p in items:
        q2 = q_ref[0, p, g * SB_TQ:(g + 1) * SB_TQ, :]
        z = lax.dot_general(q2, k_ref[0, p, key_rows(g, d), :], (((1,), (1,)), ((), ())),
                            preferred_element_type=F32)
        nlk = jnp.log(1.0 + jnp.exp2(jnp.minimum(z, SB_ZMAX))) * LOG2E
        if diag:
            nlk = jnp.where(strict, nlk, 0.0)
        z_scr[slot] = z
        n_scr[slot] = nlk.astype(BF16)

    for slot, d, diag, g, p in items:
        m = jnp.dot(n_scr[slot], cum_ref[...], preferred_element_type=F32)
        a = jnp.exp2(z_scr[slot] - m)
        if diag:
            a = jnp.where(strict, a, 0.0)
        a_scr[slot] = a.astype(BF16)
        s_scr[slot] = jnp.where(lane_a, m[:, 0:1], m[:, SB_TK:SB_TK + 1])

    rmin = None
    for c, (g, p) in enumerate(chains):
        acc, r = (None, None) if first else (acc_scr[c], r_scr[c])
        for i, d in enumerate(ds):
            slot = i * n_chain + c
            pv = jnp.dot(a_scr[slot], v_ref[0, p, key_rows(g, d), :], preferred_element_type=F32)
            if first and i == 0:
                acc, r = pv, s_scr[slot]
            else:
                r = jnp.where(qt0 + g - d >= 0, r, SB_DEAD)
                acc = acc + jnp.exp2(-r) * pv
                r = r + s_scr[slot]
        acc_scr[c] = acc
        r_scr[c] = r
        rmin = r if rmin is None else jnp.minimum(rmin, r)
    return jnp.min(rmin)


def _sb_kernel(q_ref, k_ref, v_ref, z_ref, cum_ref, o_ref, z_scr, n_scr, a_scr, s_scr, acc_scr, r_scr):
    n_sub = q_ref.shape[2] // SB_TQ
    qt0 = pl.program_id(1) * n_sub
    chains = [(g, p) for g in range(n_sub) for p in range(SB_PAIRS)]
    sweeps = functools.partial(_sb_sweeps, qt0=qt0, chains=chains, q_ref=q_ref, k_ref=k_ref,
                               v_ref=v_ref, cum_ref=cum_ref, z_scr=z_scr, n_scr=n_scr, a_scr=a_scr,
                               s_scr=s_scr, acc_scr=acc_scr, r_scr=r_scr)

    def cond(carry):
        d, rm = carry
        return jnp.logical_and(qt0 + (n_sub - 1) - d >= 0, rm < SB_SKIP)

    def body(carry):
        d, _ = carry
        return d + 1, sweeps([d], False)

    rm = sweeps(list(range(SB_MIN_SWEEPS)), True)
    lax.while_loop(cond, body, (jnp.int32(SB_MIN_SWEEPS), rm))

    for c, (g, p) in enumerate(chains):
        zc = z_ref[0, p, g * SB_TQ:(g + 1) * SB_TQ, :].astype(F32)
        o_ref[0, p, g * SB_TQ:(g + 1) * SB_TQ, :] = (acc_scr[c] * _silu(zc)).astype(BF16)


def _stick_breaking(sbq, sbk, sbv, sbz):
    bsz, pairs, s, _ = sbq.shape
    rows = min(SB_ROWS, s)
    j = lax.broadcasted_iota(jnp.int32, (2 * SB_TK, 2 * SB_TK), 0)
    c = lax.broadcasted_iota(jnp.int32, (2 * SB_TK, 2 * SB_TK), 1)
    cum = jnp.where((j >= c) & ((j // SB_TK) == (c // SB_TK)), 1.0, 0.0).astype(BF16)
    q_spec = pl.BlockSpec((1, pairs, rows, LANES), lambda b, i: (b, 0, i, 0))
    kv_spec = pl.BlockSpec((1, pairs, 2 * s, LANES), lambda b, i: (b, 0, 0, 0))
    n_chain = (rows // SB_TQ) * pairs
    n_item = SB_MIN_SWEEPS * n_chain
    return pl.pallas_call(
        _sb_kernel,
        grid=(bsz, s // rows),
        in_specs=[q_spec, kv_spec, kv_spec, q_spec,
                  pl.BlockSpec((2 * SB_TK, 2 * SB_TK), lambda b, i: (0, 0))],
        out_specs=q_spec,
        out_shape=jax.ShapeDtypeStruct((bsz, pairs, s, LANES), BF16),
        scratch_shapes=[pltpu.VMEM((n_item, SB_TQ, 2 * SB_TK), F32),
                        pltpu.VMEM((n_item, SB_TQ, 2 * SB_TK), BF16),
                        pltpu.VMEM((n_item, SB_TQ, 2 * SB_TK), BF16),
                        pltpu.VMEM((n_item, SB_TQ, LANES), F32),
                        pltpu.VMEM((n_chain, SB_TQ, LANES), F32),
                        pltpu.VMEM((n_chain, SB_TQ, LANES), F32)],
        compiler_params=pltpu.CompilerParams(
            dimension_semantics=("parallel", "arbitrary"), vmem_limit_bytes=VMEM_LIMIT),
        name="stickbreak",
    )(sbq, sbk, sbv, sbz, cum)


def _outproj_kernel(hml_ref, hsb_ref, x_ref, gate_ref, gpost_ref, wml_ref, wsb_ref, o_ref):
    hsb = jnp.concatenate([hsb_ref[0, p] for p in range(SB_PAIRS)], axis=1)
    y = (jnp.dot(hml_ref[0], wml_ref[...], preferred_element_type=F32)
         + jnp.dot(hsb, wsb_ref[...], preferred_element_type=F32))
    r = lax.rsqrt(jnp.mean(y * y, axis=-1, keepdims=True) + EPS)
    o_ref[0] = x_ref[0] + gate_ref[0] * ((y * r) * gpost_ref[...])


def _outproj(hml, hsb, x, mod3, g_post, w_oml, w_osb):
    bsz, s, d = x.shape
    tm = min(OUT_TM, s)
    return pl.pallas_call(
        _outproj_kernel,
        grid=(bsz, s // tm),
        in_specs=[pl.BlockSpec((1, tm, ML_WIDTH), lambda b, i: (b, i, 0)),
                  pl.BlockSpec((1, SB_PAIRS, tm, LANES), lambda b, i: (b, 0, i, 0)),
                  pl.BlockSpec((1, tm, d), lambda b, i: (b, i, 0)),
                  pl.BlockSpec((1, 1, d), lambda b, i: (b, 0, 2)),
                  pl.BlockSpec((1, d), lambda b, i: (0, 0)),
                  pl.BlockSpec((ML_WIDTH, d), lambda b, i: (0, 0)),
                  pl.BlockSpec((SB_WIDTH, d), lambda b, i: (0, 0))],
        out_specs=pl.BlockSpec((1, tm, d), lambda b, i: (b, i, 0)),
        out_shape=jax.ShapeDtypeStruct((bsz, s, d), F32),
        compiler_params=pltpu.CompilerParams(
            dimension_semantics=("parallel", "parallel"), vmem_limit_bytes=VMEM_LIMIT),
        name="outproj",
    )(hml, hsb, x, mod3, g_post.reshape(1, d), w_oml, w_osb)


def kernel(x, c, w_ada, b_ada, g_pre, w_in, b_igate, b_fgate, conv_w, conv_b, g_ml_head, w_out, g_post):
    bsz, s, d = x.shape
    o_gate = 2 * ML_WIDTH + 3 * ML_WIDTH
    o_sb = o_gate + 2 * ML_HEADS
    w_ml = w_in[:, :o_gate].astype(BF16)
    w_g = jnp.pad(w_in[:, o_gate:o_sb], ((0, 0), (0, LANES - 2 * ML_HEADS))).astype(BF16)
    q_scale = jnp.where(jnp.arange(4 * SB_WIDTH) < SB_WIDTH, LOG2E * SB_HEAD_DIM ** -0.5, 1.0)
    w_sb = (w_in[:, o_sb:] * q_scale).astype(BF16)
    gbias = jnp.pad(jnp.concatenate([b_igate, b_fgate]), (0, LANES - 2 * ML_HEADS)).reshape(1, LANES)
    w_o = w_out.astype(BF16)

    mod3 = _adaln(c, w_ada, b_ada).reshape(bsz, 1, 3 * d)
    w_vt = w_in[:, 2 * ML_WIDTH:3 * ML_WIDTH].T.astype(BF16)
    ml, vt, sbq, sbk, sbv, sbz, gates = _inproj(x, mod3, g_pre, w_ml, w_vt, w_sb, w_g)
    hml = _mlstm(ml, vt, gates, conv_w, conv_b, gbias, g_ml_head)
    hsb = _stick_breaking(sbq, sbk, sbv, sbz)
    return _outproj(hml, hsb, x, mod3, g_post, w_o[:ML_WIDTH], w_o[ML_WIDTH:])
```

```python
import functools

import jax
import jax.numpy as jnp
from jax import lax
from jax.experimental import pallas as pl
from jax.experimental.pallas import tpu as pltpu

F32 = jnp.float32
BF16 = jnp.bfloat16
HIGHEST = lax.Precision.HIGHEST

EPS = 1e-6
ML_HEADS = 4
ML_HEAD_DIM = 128
ML_WIDTH = ML_HEADS * ML_HEAD_DIM
SB_HEADS = 8
SB_HEAD_DIM = 64
SB_WIDTH = SB_HEADS * SB_HEAD_DIM
SB_PAIRS = SB_HEADS // 2
CONV_K = 4
CHUNK = 64
LANES = 128
SUBLANES = 8

IN_TM = 512
ML_T = 256
SB_TQ = 128
SB_TK = 128
SB_ROWS = 256
OUT_TM = 1024
LOG2E = 1.4426950408889634
SB_SKIP = 160.0
SB_DEAD = 1e30
SB_MIN_SWEEPS = 3
SB_ZMAX = 126.0
VMEM_LIMIT = 56 * 1024 * 1024


def _sigmoid(x):
    return 0.5 + 0.5 * jnp.tanh(0.5 * x)


def _silu(x):
    hx = 0.5 * x
    return hx + hx * jnp.tanh(hx)


def _softplus(x):
    return jnp.maximum(x, 0.0) + jnp.log(1.0 + jnp.exp(-jnp.abs(x)))


def _adaln_kernel(c_ref, w_ref, b_ref, o_ref):
    c = c_ref[...]
    s = _silu(c)
    o_ref[...] = jnp.dot(s, w_ref[...], preferred_element_type=F32, precision=HIGHEST) + b_ref[...]


def _adaln(c, w_ada, b_ada):
    bsz, d = c.shape
    n = w_ada.shape[1]
    return pl.pallas_call(
        _adaln_kernel,
        grid=(n // d,),
        in_specs=[pl.BlockSpec((bsz, d), lambda j: (0, 0)),
                  pl.BlockSpec((d, d), lambda j: (0, j)),
                  pl.BlockSpec((1, d), lambda j: (0, j))],
        out_specs=pl.BlockSpec((bsz, d), lambda j: (0, j)),
        out_shape=jax.ShapeDtypeStruct((bsz, n), F32),
        name="adaln",
    )(c, w_ada, b_ada.reshape(1, n))


def _inproj_kernel(x_ref, shift_ref, scale_ref, gpre_ref, wml_ref, wvt_ref, wsb_ref, wg_ref,
                   ml_ref, vt_ref, sbq_ref, sbk_ref, sbv_ref, sbz_ref, gates_ref, h_scr):
    x = x_ref[0]
    r = lax.rsqrt(jnp.mean(x * x, axis=-1, keepdims=True) + EPS)
    a = gpre_ref[...] * (1.0 + scale_ref[0])
    h_scr[...] = ((x * r) * a + shift_ref[0]).astype(BF16)

    n_ml = wml_ref.shape[1]
    for n in range(n_ml // 512):
        cols = slice(n * 512, (n + 1) * 512)
        ml_ref[0, :, cols] = jnp.dot(h_scr[...], wml_ref[:, cols],
                                     preferred_element_type=F32).astype(BF16)
    tm = x.shape[0]
    vt = lax.dot_general(wvt_ref[...], h_scr[...], (((1,), (1,)), ((), ())),
                         preferred_element_type=F32).astype(BF16)
    for j in range(tm // CHUNK):
        vt_ref[0, j] = vt[:, j * CHUNK:(j + 1) * CHUNK]
    half = lax.broadcasted_iota(jnp.int32, (tm, LANES), 1) < SB_HEAD_DIM
    for n, dst in enumerate((sbq_ref, sbk_ref, sbv_ref, sbz_ref)):
        cols = slice(n * SB_WIDTH, (n + 1) * SB_WIDTH)
        res = jnp.dot(h_scr[...], wsb_ref[:, cols], preferred_element_type=F32).astype(BF16)
        for p in range(SB_PAIRS):
            blk = res[:, p * LANES:(p + 1) * LANES]
            if dst is sbk_ref or dst is sbv_ref:
                zero = jnp.zeros_like(blk)
                only_a = jnp.where(half, blk, zero)
                only_b = jnp.where(half, zero, blk)
                for j in range(tm // SB_TK):
                    rows = slice(j * SB_TK, (j + 1) * SB_TK)
                    dst[0, p, 2 * j * SB_TK:(2 * j + 1) * SB_TK, :] = only_a[rows]
                    dst[0, p, (2 * j + 1) * SB_TK:(2 * j + 2) * SB_TK, :] = only_b[rows]
            else:
                dst[0, p] = blk
    gates_ref[0] = jnp.dot(h_scr[...], wg_ref[...], preferred_element_type=F32)


def _inproj(x, mod3, g_pre, w_ml, w_vt, w_sb, w_g):
    bsz, s, d = x.shape
    tm = min(IN_TM, s)
    n_ml = w_ml.shape[1]
    vt_shape = jax.ShapeDtypeStruct((bsz, s // CHUNK, ML_WIDTH, CHUNK), BF16)
    vt_spec = pl.BlockSpec((1, tm // CHUNK, ML_WIDTH, CHUNK), lambda b, i: (b, i, 0, 0))
    sb_shape = jax.ShapeDtypeStruct((bsz, SB_PAIRS, s, LANES), BF16)
    sb_spec = pl.BlockSpec((1, SB_PAIRS, tm, LANES), lambda b, i: (b, 0, i, 0))
    kv_shape = jax.ShapeDtypeStruct((bsz, SB_PAIRS, 2 * s, LANES), BF16)
    kv_spec = pl.BlockSpec((1, SB_PAIRS, 2 * tm, LANES), lambda b, i: (b, 0, i, 0))
    return pl.pallas_call(
        _inproj_kernel,
        grid=(bsz, s // tm),
        in_specs=[pl.BlockSpec((1, tm, d), lambda b, i: (b, i, 0)),
                  pl.BlockSpec((1, 1, d), lambda b, i: (b, 0, 0)),
                  pl.BlockSpec((1, 1, d), lambda b, i: (b, 0, 1)),
                  pl.BlockSpec((1, d), lambda b, i: (0, 0)),
                  pl.BlockSpec((d, n_ml), lambda b, i: (0, 0)),
                  pl.BlockSpec((ML_WIDTH, d), lambda b, i: (0, 0)),
                  pl.BlockSpec((d, 4 * SB_WIDTH), lambda b, i: (0, 0)),
                  pl.BlockSpec((d, LANES), lambda b, i: (0, 0))],
        out_specs=[pl.BlockSpec((1, tm, n_ml), lambda b, i: (b, i, 0)),
                   vt_spec, sb_spec, kv_spec, kv_spec, sb_spec,
                   pl.BlockSpec((1, tm, LANES), lambda b, i: (b, i, 0))],
        out_shape=[jax.ShapeDtypeStruct((bsz, s, n_ml), BF16),
                   vt_shape, sb_shape, kv_shape, kv_shape, sb_shape,
                   jax.ShapeDtypeStruct((bsz, s, LANES), F32)],
        scratch_shapes=[pltpu.VMEM((tm, d), BF16)],
        compiler_params=pltpu.CompilerParams(
            dimension_semantics=("parallel", "parallel"), vmem_limit_bytes=VMEM_LIMIT),
        name="inproj",
    )(x, mod3, mod3, g_pre.reshape(1, d), w_ml, w_vt, w_sb, w_g)


def _mlstm_kernel(u_ref, v_ref, vt_ref, o_ref, z_ref, g_ref, convw_ref, convb_ref, gbias_ref,
                  ghead_ref, out_ref, ubuf, q_scr, k_scr, bc_scr, qkm_scr, cl_scr, nl_scr, cp_scr,
                  hg_scr, h2_scr, c_scr, n_scr, m_scr):
    t = u_ref.shape[1]
    n_chunks = t // CHUNK
    units = [(c, h) for c in range(n_chunks) for h in range(ML_HEADS)]
    j = pl.program_id(1)

    @pl.when(j == 0)
    def _():
        ubuf[0:SUBLANES, :] = jnp.zeros((SUBLANES, ubuf.shape[1]), F32)
        c_scr[...] = jnp.zeros_like(c_scr)
        n_scr[...] = jnp.zeros_like(n_scr)
        m_scr[...] = jnp.zeros_like(m_scr)

    @pl.when(j > 0)
    def _():
        ubuf[0:SUBLANES, :] = ubuf[t:t + SUBLANES, :]

    ubuf[SUBLANES:SUBLANES + t, :] = u_ref[0].astype(F32)

    conv = convb_ref[...]
    for i in range(CONV_K):
        off = SUBLANES - (CONV_K - 1) + i
        conv = conv + convw_ref[i:i + 1, :] * ubuf[off:off + t, :]
    qk = _silu(conv)
    q_scr[...] = qk[:, :ML_WIDTH].astype(BF16)
    k_scr[...] = qk[:, ML_WIDTH:] * (ML_HEAD_DIM ** -0.5)

    gates = g_ref[0] + gbias_ref[...]
    pos = lax.broadcasted_iota(jnp.int32, (t, LANES), 0) & (CHUNK - 1)
    bcum = -_softplus(-gates)
    step = 1
    while step < CHUNK:
        bcum = bcum + jnp.where(pos >= step, pltpu.roll(bcum, step, 0), 0.0)
        step *= 2
    bcum = pltpu.roll(bcum, LANES - ML_HEADS, 1)
    cdiff = gates - bcum
    cm = cdiff
    step = 1
    while step < CHUNK:
        cm = jnp.maximum(cm, jnp.where(pos >= step, pltpu.roll(cm, step, 0), -jnp.inf))
        step *= 2
    c_t = cdiff.T[0:SUBLANES, :]

    m_run = m_scr[0:1, :]
    s_old, s_loc, stats = [], [], []
    for c in range(n_chunks):
        rows = slice(c * CHUNK, (c + 1) * CHUNK)
        b_c = bcum[rows]
        a_c = b_c[CHUNK - 1:CHUNK, :]
        g_c = a_c - b_c + gates[rows]
        m_loc = jnp.max(g_c, axis=0, keepdims=True)
        mm = jnp.maximum(m_run, cm[rows])
        stats.append((mm, jnp.exp(m_run - mm), jnp.exp(-(b_c + mm)), jnp.exp(g_c - m_loc)))
        m_new = jnp.maximum(a_c + m_run, m_loc)
        s_old.append(jnp.exp(a_c + m_run - m_new))
        s_loc.append(jnp.exp(m_loc - m_new))
        m_run = m_new
    m_scr[0:1, :] = m_run

    for u, (c, h) in enumerate(units):
        for i, stat in enumerate(stats[c]):
            bc_scr[u, i] = jnp.broadcast_to(stat[:, h:h + 1], (CHUNK, LANES))

    causal = (lax.broadcasted_iota(jnp.int32, (CHUNK, CHUNK), 0)
              >= lax.broadcasted_iota(jnp.int32, (CHUNK, CHUNK), 1))
    ones_rows = jnp.ones((2 * SUBLANES, CHUNK), BF16)

    for u, (c, h) in enumerate(units):
        rows = slice(c * CHUNK, (c + 1) * CHUNK)
        lanes = slice(h * ML_HEAD_DIM, (h + 1) * ML_HEAD_DIM)
        k_c = k_scr[rows, lanes]
        s_qk = lax.dot_general(q_scr[rows, lanes], k_c.astype(BF16), (((1,), (1,)), ((), ())),
                               preferred_element_type=F32)
        dw = jnp.where(causal, jnp.exp(c_t[h:h + 1, rows] - bc_scr[u, 0][:, :CHUNK]), 0.0)
        qkm_scr[u] = (s_qk * dw).astype(BF16)
        wk = (bc_scr[u, 3] * k_c).astype(BF16)
        lhs = jnp.concatenate([vt_ref[0, c, lanes, :], ones_rows], axis=0)
        res = jnp.dot(lhs, wk, preferred_element_type=F32)
        cl_scr[u] = res[:ML_HEAD_DIM]
        nl_scr[u, 0:1, :] = res[ML_HEAD_DIM:ML_HEAD_DIM + 1]

    for h in range(ML_HEADS):
        c_run = c_scr[h]
        n_run = n_scr[h:h + 1, :]
        for c in range(n_chunks):
            u = c * ML_HEADS + h
            cp_scr[u, 0:ML_HEAD_DIM, :] = c_run.astype(BF16)
            cp_scr[u, ML_HEAD_DIM:2 * ML_HEAD_DIM, :] = jnp.broadcast_to(
                n_run, (ML_HEAD_DIM, ML_HEAD_DIM)).astype(BF16)
            so = s_old[c][:, h:h + 1]
            sl = s_loc[c][:, h:h + 1]
            c_run = so * c_run + sl * cl_scr[u]
            n_run = so * n_run + sl * nl_scr[u, 0:1, :]
        c_scr[h] = c_run
        n_scr[h:h + 1, :] = n_run

    ones_cols = jnp.ones((CHUNK, ML_HEAD_DIM), BF16)
    for u, (c, h) in enumerate(units):
        rows = slice(c * CHUNK, (c + 1) * CHUNK)
        lanes = slice(h * ML_HEAD_DIM, (h + 1) * ML_HEAD_DIM)
        qc = lax.dot_general(q_scr[rows, lanes], cp_scr[u], (((1,), (1,)), ((), ())),
                             preferred_element_type=F32)
        kv = jnp.dot(qkm_scr[u], jnp.concatenate([v_ref[0, rows, lanes], ones_cols], axis=1),
                     preferred_element_type=F32)
        inter_w = bc_scr[u, 1]
        den = inter_w * qc[:, ML_HEAD_DIM:] + kv[:, ML_HEAD_DIM:]
        num = inter_w * qc[:, :ML_HEAD_DIM] + kv[:, :ML_HEAD_DIM]
        hh = num * (1.0 / jnp.maximum(jnp.abs(den), bc_scr[u, 2]))
        hg = _sigmoid(o_ref[0, rows, lanes].astype(F32)) * hh
        hg_scr[u] = hg
        h2_scr[u] = (hg * hg).astype(BF16)

    ones_sq = jnp.ones((ML_HEAD_DIM, ML_HEAD_DIM), BF16)
    for u, (c, h) in enumerate(units):
        rows = slice(c * CHUNK, (c + 1) * CHUNK)
        lanes = slice(h * ML_HEAD_DIM, (h + 1) * ML_HEAD_DIM)
        ss = jnp.dot(h2_scr[u], ones_sq, preferred_element_type=F32)
        hn = hg_scr[u] * lax.rsqrt(ss * (1.0 / ML_HEAD_DIM) + EPS) * ghead_ref[:, lanes]
        out_ref[0, rows, lanes] = (hn * _silu(z_ref[0, rows, lanes].astype(F32))).astype(BF16)


def _mlstm(ml, vt, gates, conv_w, conv_b, gbias, g_head):
    bsz, s, _ = ml.shape
    t = min(ML_T, s)
    w2 = 2 * ML_WIDTH
    n_units = (t // CHUNK) * ML_HEADS
    return pl.pallas_call(
        _mlstm_kernel,
        grid=(bsz, s // t),
        in_specs=[pl.BlockSpec((1, t, w2), lambda b, j: (b, j, 0)),
                  pl.BlockSpec((1, t, ML_WIDTH), lambda b, j: (b, j, 2)),
                  pl.BlockSpec((1, t // CHUNK, ML_WIDTH, CHUNK), lambda b, j: (b, j, 0, 0)),
                  pl.BlockSpec((1, t, ML_WIDTH), lambda b, j: (b, j, 3)),
                  pl.BlockSpec((1, t, ML_WIDTH), lambda b, j: (b, j, 4)),
                  pl.BlockSpec((1, t, LANES), lambda b, j: (b, j, 0)),
                  pl.BlockSpec((CONV_K, w2), lambda b, j: (0, 0)),
                  pl.BlockSpec((1, w2), lambda b, j: (0, 0)),
                  pl.BlockSpec((1, LANES), lambda b, j: (0, 0)),
                  pl.BlockSpec((1, ML_WIDTH), lambda b, j: (0, 0))],
        out_specs=pl.BlockSpec((1, t, ML_WIDTH), lambda b, j: (b, j, 0)),
        out_shape=jax.ShapeDtypeStruct((bsz, s, ML_WIDTH), BF16),
        scratch_shapes=[pltpu.VMEM((t + 2 * SUBLANES, w2), F32),
                        pltpu.VMEM((t, ML_WIDTH), BF16),
                        pltpu.VMEM((t, ML_WIDTH), F32),
                        pltpu.VMEM((n_units, 4, CHUNK, LANES), F32),
                        pltpu.VMEM((n_units, CHUNK, CHUNK), BF16),
                        pltpu.VMEM((n_units, ML_HEAD_DIM, ML_HEAD_DIM), F32),
                        pltpu.VMEM((n_units, SUBLANES, ML_HEAD_DIM), F32),
                        pltpu.VMEM((n_units, 2 * ML_HEAD_DIM, ML_HEAD_DIM), BF16),
                        pltpu.VMEM((n_units, CHUNK, ML_HEAD_DIM), F32),
                        pltpu.VMEM((n_units, CHUNK, ML_HEAD_DIM), BF16),
                        pltpu.VMEM((ML_HEADS, ML_HEAD_DIM, ML_HEAD_DIM), F32),
                        pltpu.VMEM((SUBLANES, ML_HEAD_DIM), F32),
                        pltpu.VMEM((SUBLANES, LANES), F32)],
        compiler_params=pltpu.CompilerParams(
            dimension_semantics=("parallel", "arbitrary"), vmem_limit_bytes=VMEM_LIMIT),
        name="mlstm",
    )(ml, ml, vt, ml, ml, gates, conv_w, conv_b.reshape(1, w2), gbias, g_head.reshape(1, ML_WIDTH))


def _replace_rows(full, part, r0):
    pieces = [full[:r0]] if r0 else []
    pieces.append(part)
    if r0 + part.shape[0] < full.shape[0]:
        pieces.append(full[r0 + part.shape[0]:])
    return pieces[0] if len(pieces) == 1 else jnp.concatenate(pieces, axis=0)


def _sb_sweeps(plan, first, qt0, chains, q_ref, k_ref, v_ref, cum_ref, z_scr, n_scr, a_scr, s_scr,
               acc_scr, r_scr):
    n_chain = len(chains)
    items = [(i * n_chain + c, d, r0, nr, first and i == 0, g, p)
             for i, (d, r0, nr) in enumerate(plan) for c, (g, p) in enumerate(chains)]
    if first:
        strict = ((lax.broadcasted_iota(jnp.int32, (SB_TQ, 2 * SB_TK), 1) & (SB_TK - 1))
                  < lax.broadcasted_iota(jnp.int32, (SB_TQ, 2 * SB_TK), 0))

    def key_rows(g, d):
        return pl.ds(jnp.maximum(qt0 + g - d, 0) * (2 * SB_TK), 2 * SB_TK)

    for slot, d, r0, nr, diag, g, p in items:
        q2 = q_ref[0, p, g * SB_TQ + r0:g * SB_TQ + r0 + nr, :]
        z = lax.dot_general(q2, k_ref[0, p, key_rows(g, d), :], (((1,), (1,)), ((), ())),
                            preferred_element_type=F32)
        nlk = jnp.log(1.0 + jnp.exp2(jnp.minimum(z, SB_ZMAX))) * LOG2E
        if diag:
            nlk = jnp.where(strict, nlk, 0.0)
        z_scr[slot, 0:nr, :] = z
        n_scr[slot, 0:nr, :] = nlk.astype(BF16)

    for slot, d, r0, nr, diag, g, p in items:
        m = jnp.dot(n_scr[slot, 0:nr, :], cum_ref[...], preferred_element_type=F32)
        a = jnp.exp2(z_scr[slot, 0:nr, :] - m)
        if diag:
            a = jnp.where(strict, a, 0.0)
        a_scr[slot, 0:nr, :] = a.astype(BF16)
        lane_a = lax.broadcasted_iota(jnp.int32, (nr, LANES), 1) < SB_HEAD_DIM
        s_scr[slot, 0:nr, :] = jnp.where(lane_a, m[:, 0:1], m[:, SB_TK:SB_TK + 1])

    rmin = None
    for c, (g, p) in enumerate(chains):
        acc, r = (None, None) if first else (acc_scr[c], r_scr[c])
        for i, (d, r0, nr) in enumerate(plan):
            slot = i * n_chain + c
            pv = jnp.dot(a_scr[slot, 0:nr, :], v_ref[0, p, key_rows(g, d), :],
                         preferred_element_type=F32)
            if first and i == 0:
                acc, r = pv, s_scr[slot]
            else:
                r_part = jnp.where(qt0 + g - d >= 0, r[r0:r0 + nr], SB_DEAD)
                acc_part = acc[r0:r0 + nr] + jnp.exp2(-r_part) * pv
                r_part = r_part + s_scr[slot, 0:nr, :]
                acc, r = (_replace_rows(acc, acc_part, r0), _replace_rows(r, r_part, r0))
        acc_scr[c] = acc
        r_scr[c] = r
        rmin = r if rmin is None else jnp.minimum(rmin, r)
    return jnp.min(rmin), jnp.min(rmin[SB_TQ // 2:])


def _sb_kernel(q_ref, k_ref, v_ref, z_ref, cum_ref, o_ref, z_scr, n_scr, a_scr, s_scr, acc_scr, r_scr):
    n_sub = q_ref.shape[2] // SB_TQ
    qt0 = pl.program_id(1) * n_sub
    chains = [(g, p) for g in range(n_sub) for p in range(SB_PAIRS)]
    sweeps = functools.partial(_sb_sweeps, qt0=qt0, chains=chains, q_ref=q_ref, k_ref=k_ref,
                               v_ref=v_ref, cum_ref=cum_ref, z_scr=z_scr, n_scr=n_scr, a_scr=a_scr,
                               s_scr=s_scr, acc_scr=acc_scr, r_scr=r_scr)
    half = SB_TQ // 2

    rm_all, rm_low = sweeps([(0, 0, SB_TQ), (1, 0, SB_TQ), (2, 0, half)], True)

    @pl.when(rm_low < SB_SKIP)
    def _():
        sweeps([(2, half, half)], False)

    def cond(carry):
        d, rm = carry
        return jnp.logical_and(qt0 + (n_sub - 1) - d >= 0, rm < SB_SKIP)

    def body(carry):
        d, _ = carry
        return d + 1, sweeps([(d, 0, SB_TQ)], False)[0]

    lax.while_loop(cond, body, (jnp.int32(SB_MIN_SWEEPS), rm_all))

    for c, (g, p) in enumerate(chains):
        zc = z_ref[0, p, g * SB_TQ:(g + 1) * SB_TQ, :].astype(F32)
        o_ref[0, p, g * SB_TQ:(g + 1) * SB_TQ, :] = (acc_scr[c] * _silu(zc)).astype(BF16)


def _stick_breaking(sbq, sbk, sbv, sbz):
    bsz, pairs, s, _ = sbq.shape
    rows = min(SB_ROWS, s)
    j = lax.broadcasted_iota(jnp.int32, (2 * SB_TK, 2 * SB_TK), 0)
    c = lax.broadcasted_iota(jnp.int32, (2 * SB_TK, 2 * SB_TK), 1)
    cum = jnp.where((j >= c) & ((j // SB_TK) == (c // SB_TK)), 1.0, 0.0).astype(BF16)
    q_spec = pl.BlockSpec((1, pairs, rows, LANES), lambda b, i: (b, 0, i, 0))
    kv_spec = pl.BlockSpec((1, pairs, 2 * s, LANES), lambda b, i: (b, 0, 0, 0))
    n_chain = (rows // SB_TQ) * pairs
    n_item = SB_MIN_SWEEPS * n_chain
    return pl.pallas_call(
        _sb_kernel,
        grid=(bsz, s // rows),
        in_specs=[q_spec, kv_spec, kv_spec, q_spec,
                  pl.BlockSpec((2 * SB_TK, 2 * SB_TK), lambda b, i: (0, 0))],
        out_specs=q_spec,
        out_shape=jax.ShapeDtypeStruct((bsz, pairs, s, LANES), BF16),
        scratch_shapes=[pltpu.VMEM((n_item, SB_TQ, 2 * SB_TK), F32),
                        pltpu.VMEM((n_item, SB_TQ, 2 * SB_TK), BF16),
                        pltpu.VMEM((n_item, SB_TQ, 2 * SB_TK), BF16),
                        pltpu.VMEM((n_item, SB_TQ, LANES), F32),
                        pltpu.VMEM((n_chain, SB_TQ, LANES), F32),
                        pltpu.VMEM((n_chain, SB_TQ, LANES), F32)],
        compiler_params=pltpu.CompilerParams(
            dimension_semantics=("parallel", "arbitrary"), vmem_limit_bytes=VMEM_LIMIT),
        name="stickbreak",
    )(sbq, sbk, sbv, sbz, cum)


def _outproj_kernel(hml_ref, hsb_ref, x_ref, gate_ref, gpost_ref, wml_ref, wsb_ref, o_ref):
    hsb = jnp.concatenate([hsb_ref[0, p] for p in range(SB_PAIRS)], axis=1)
    y = (jnp.dot(hml_ref[0], wml_ref[...], preferred_element_type=F32)
         + jnp.dot(hsb, wsb_ref[...], preferred_element_type=F32))
    r = lax.rsqrt(jnp.mean(y * y, axis=-1, keepdims=True) + EPS)
    o_ref[0] = x_ref[0] + gate_ref[0] * ((y * r) * gpost_ref[...])


def _outproj(hml, hsb, x, mod3, g_post, w_oml, w_osb):
    bsz, s, d = x.shape
    tm = min(OUT_TM, s)
    return pl.pallas_call(
        _outproj_kernel,
        grid=(bsz, s // tm),
        in_specs=[pl.BlockSpec((1, tm, ML_WIDTH), lambda b, i: (b, i, 0)),
                  pl.BlockSpec((1, SB_PAIRS, tm, LANES), lambda b, i: (b, 0, i, 0)),
                  pl.BlockSpec((1, tm, d), lambda b, i: (b, i, 0)),
                  pl.BlockSpec((1, 1, d), lambda b, i: (b, 0, 2)),
                  pl.BlockSpec((1, d), lambda b, i: (0, 0)),
                  pl.BlockSpec((ML_WIDTH, d), lambda b, i: (0, 0)),
                  pl.BlockSpec((SB_WIDTH, d), lambda b, i: (0, 0))],
        out_specs=pl.BlockSpec((1, tm, d), lambda b, i: (b, i, 0)),
        out_shape=jax.ShapeDtypeStruct((bsz, s, d), F32),
        compiler_params=pltpu.CompilerParams(
            dimension_semantics=("parallel", "parallel"), vmem_limit_bytes=VMEM_LIMIT),
        name="outproj",
    )(hml, hsb, x, mod3, g_post.reshape(1, d), w_oml, w_osb)


def kernel(x, c, w_ada, b_ada, g_pre, w_in, b_igate, b_fgate, conv_w, conv_b, g_ml_head, w_out, g_post):
    bsz, s, d = x.shape
    o_gate = 2 * ML_WIDTH + 3 * ML_WIDTH
    o_sb = o_gate + 2 * ML_HEADS
    w_ml = w_in[:, :o_gate].astype(BF16)
    w_g = jnp.pad(w_in[:, o_gate:o_sb], ((0, 0), (0, LANES - 2 * ML_HEADS))).astype(BF16)
    q_scale = jnp.where(jnp.arange(4 * SB_WIDTH) < SB_WIDTH, LOG2E * SB_HEAD_DIM ** -0.5, 1.0)
    w_sb = (w_in[:, o_sb:] * q_scale).astype(BF16)
    gbias = jnp.pad(jnp.concatenate([b_igate, b_fgate]), (0, LANES - 2 * ML_HEADS)).reshape(1, LANES)
    w_o = w_out.astype(BF16)

    mod3 = _adaln(c, w_ada, b_ada).reshape(bsz, 1, 3 * d)
    w_vt = w_in[:, 2 * ML_WIDTH:3 * ML_WIDTH].T.astype(BF16)
    ml, vt, sbq, sbk, sbv, sbz, gates = _inproj(x, mod3, g_pre, w_ml, w_vt, w_sb, w_g)
    hml = _mlstm(ml, vt, gates, conv_w, conv_b, gbias, g_ml_head)
    hsb = _stick_breaking(sbq, sbk, sbv, sbz)
    return _outproj(hml, hsb, x, mod3, g_post, w_o[:ML_WIDTH], w_o[ML_WIDTH:])
```

```python
import functools

import jax
import jax.numpy as jnp
from jax import lax
from jax.experimental import pallas as pl
from jax.experimental.pallas import tpu as pltpu

F32 = jnp.float32
BF16 = jnp.bfloat16
HIGHEST = lax.Precision.HIGHEST

EPS = 1e-6
ML_HEADS = 4
ML_HEAD_DIM = 128
ML_WIDTH = ML_HEADS * ML_HEAD_DIM
SB_HEADS = 8
SB_HEAD_DIM = 64
SB_WIDTH = SB_HEADS * SB_HEAD_DIM
SB_PAIRS = SB_HEADS // 2
CONV_K = 4
CHUNK = 64
LANES = 128
SUBLANES = 8

IN_TM = 512
ML_T = 256
SB_TQ = 128
SB_TK = 128
SB_ROWS = 256
OUT_TM = 1024
LOG2E = 1.4426950408889634
SB_SKIP = 128.0
SB_DEAD = 1e30
SB_MIN_SWEEPS = 3
SB_ZMAX = 126.0
VMEM_LIMIT = 56 * 1024 * 1024


def _sigmoid(x):
    return 0.5 + 0.5 * jnp.tanh(0.5 * x)


def _silu(x):
    hx = 0.5 * x
    return hx + hx * jnp.tanh(hx)


def _softplus(x):
    return jnp.maximum(x, 0.0) + jnp.log(1.0 + jnp.exp(-jnp.abs(x)))


def _adaln_kernel(c_ref, w_ref, b_ref, o_ref):
    c = c_ref[...]
    s = _silu(c)
    o_ref[...] = jnp.dot(s, w_ref[...], preferred_element_type=F32, precision=HIGHEST) + b_ref[...]


def _adaln(c, w_ada, b_ada):
    bsz, d = c.shape
    n = w_ada.shape[1]
    return pl.pallas_call(
        _adaln_kernel,
        grid=(n // d,),
        in_specs=[pl.BlockSpec((bsz, d), lambda j: (0, 0)),
                  pl.BlockSpec((d, d), lambda j: (0, j)),
                  pl.BlockSpec((1, d), lambda j: (0, j))],
        out_specs=pl.BlockSpec((bsz, d), lambda j: (0, j)),
        out_shape=jax.ShapeDtypeStruct((bsz, n), F32),
        name="adaln",
    )(c, w_ada, b_ada.reshape(1, n))


def _inproj_kernel(x_ref, shift_ref, scale_ref, gpre_ref, wml_ref, wvt_ref, wsb_ref, wg_ref,
                   ml_ref, vt_ref, sbq_ref, sbk_ref, sbv_ref, sbz_ref, gates_ref, h_scr):
    x = x_ref[0]
    r = lax.rsqrt(jnp.mean(x * x, axis=-1, keepdims=True) + EPS)
    a = gpre_ref[...] * (1.0 + scale_ref[0])
    h_scr[...] = ((x * r) * a + shift_ref[0]).astype(BF16)

    n_ml = wml_ref.shape[1]
    for n in range(n_ml // 512):
        cols = slice(n * 512, (n + 1) * 512)
        ml_ref[0, :, cols] = jnp.dot(h_scr[...], wml_ref[:, cols],
                                     preferred_element_type=F32).astype(BF16)
    tm = x.shape[0]
    vt = lax.dot_general(wvt_ref[...], h_scr[...], (((1,), (1,)), ((), ())),
                         preferred_element_type=F32).astype(BF16)
    for j in range(tm // CHUNK):
        vt_ref[0, j] = vt[:, j * CHUNK:(j + 1) * CHUNK]
    half = lax.broadcasted_iota(jnp.int32, (tm, LANES), 1) < SB_HEAD_DIM
    for n, dst in enumerate((sbq_ref, sbk_ref, sbv_ref, sbz_ref)):
        cols = slice(n * SB_WIDTH, (n + 1) * SB_WIDTH)
        res = jnp.dot(h_scr[...], wsb_ref[:, cols], preferred_element_type=F32).astype(BF16)
        for p in range(SB_PAIRS):
            blk = res[:, p * LANES:(p + 1) * LANES]
            if dst is sbk_ref or dst is sbv_ref:
                zero = jnp.zeros_like(blk)
                only_a = jnp.where(half, blk, zero)
                only_b = jnp.where(half, zero, blk)
                for j in range(tm // SB_TK):
                    rows = slice(j * SB_TK, (j + 1) * SB_TK)
                    dst[0, p, 2 * j * SB_TK:(2 * j + 1) * SB_TK, :] = only_a[rows]
                    dst[0, p, (2 * j + 1) * SB_TK:(2 * j + 2) * SB_TK, :] = only_b[rows]
            else:
                dst[0, p] = blk
    gates_ref[0] = jnp.dot(h_scr[...], wg_ref[...], preferred_element_type=F32)


def _inproj(x, mod3, g_pre, w_ml, w_vt, w_sb, w_g):
    bsz, s, d = x.shape
    tm = min(IN_TM, s)
    n_ml = w_ml.shape[1]
    vt_shape = jax.ShapeDtypeStruct((bsz, s // CHUNK, ML_WIDTH, CHUNK), BF16)
    vt_spec = pl.BlockSpec((1, tm // CHUNK, ML_WIDTH, CHUNK), lambda b, i: (b, i, 0, 0))
    sb_shape = jax.ShapeDtypeStruct((bsz, SB_PAIRS, s, LANES), BF16)
    sb_spec = pl.BlockSpec((1, SB_PAIRS, tm, LANES), lambda b, i: (b, 0, i, 0))
    kv_shape = jax.ShapeDtypeStruct((bsz, SB_PAIRS, 2 * s, LANES), BF16)
    kv_spec = pl.BlockSpec((1, SB_PAIRS, 2 * tm, LANES), lambda b, i: (b, 0, i, 0))
    return pl.pallas_call(
        _inproj_kernel,
        grid=(bsz, s // tm),
        in_specs=[pl.BlockSpec((1, tm, d), lambda b, i: (b, i, 0)),
                  pl.BlockSpec((1, 1, d), lambda b, i: (b, 0, 0)),
                  pl.BlockSpec((1, 1, d), lambda b, i: (b, 0, 1)),
                  pl.BlockSpec((1, d), lambda b, i: (0, 0)),
                  pl.BlockSpec((d, n_ml), lambda b, i: (0, 0)),
                  pl.BlockSpec((ML_WIDTH, d), lambda b, i: (0, 0)),
                  pl.BlockSpec((d, 4 * SB_WIDTH), lambda b, i: (0, 0)),
                  pl.BlockSpec((d, LANES), lambda b, i: (0, 0))],
        out_specs=[pl.BlockSpec((1, tm, n_ml), lambda b, i: (b, i, 0)),
                   vt_spec, sb_spec, kv_spec, kv_spec, sb_spec,
                   pl.BlockSpec((1, tm, LANES), lambda b, i: (b, i, 0))],
        out_shape=[jax.ShapeDtypeStruct((bsz, s, n_ml), BF16),
                   vt_shape, sb_shape, kv_shape, kv_shape, sb_shape,
                   jax.ShapeDtypeStruct((bsz, s, LANES), F32)],
        scratch_shapes=[pltpu.VMEM((tm, d), BF16)],
        compiler_params=pltpu.CompilerParams(
            dimension_semantics=("parallel", "parallel"), vmem_limit_bytes=VMEM_LIMIT),
        name="inproj",
    )(x, mod3, mod3, g_pre.reshape(1, d), w_ml, w_vt, w_sb, w_g)


def _mlstm_kernel(u_ref, v_ref, vt_ref, o_ref, z_ref, g_ref, convw_ref, convb_ref, gbias_ref,
                  ghead_ref, out_ref, ubuf, q_scr, k_scr, bc_scr, qkm_scr, cl_scr, nl_scr, cp_scr,
                  hg_scr, h2_scr, c_scr, n_scr, m_scr):
    t = u_ref.shape[1]
    n_chunks = t // CHUNK
    units = [(c, h) for c in range(n_chunks) for h in range(ML_HEADS)]
    j = pl.program_id(1)

    @pl.when(j == 0)
    def _():
        ubuf[0:SUBLANES, :] = jnp.zeros((SUBLANES, ubuf.shape[1]), F32)
        c_scr[...] = jnp.zeros_like(c_scr)
        n_scr[...] = jnp.zeros_like(n_scr)
        m_scr[...] = jnp.zeros_like(m_scr)

    @pl.when(j > 0)
    def _():
        ubuf[0:SUBLANES, :] = ubuf[t:t + SUBLANES, :]

    ubuf[SUBLANES:SUBLANES + t, :] = u_ref[0].astype(F32)

    conv = convb_ref[...]
    for i in range(CONV_K):
        off = SUBLANES - (CONV_K - 1) + i
        conv = conv + convw_ref[i:i + 1, :] * ubuf[off:off + t, :]
    qk = _silu(conv)
    q_scr[...] = qk[:, :ML_WIDTH].astype(BF16)
    k_scr[...] = qk[:, ML_WIDTH:] * (ML_HEAD_DIM ** -0.5)

    gates = g_ref[0] + gbias_ref[...]
    pos = lax.broadcasted_iota(jnp.int32, (t, LANES), 0) & (CHUNK - 1)
    bcum = -_softplus(-gates)
    step = 1
    while step < CHUNK:
        bcum = bcum + jnp.where(pos >= step, pltpu.roll(bcum, step, 0), 0.0)
        step *= 2
    bcum = pltpu.roll(bcum, LANES - ML_HEADS, 1)
    cdiff = gates - bcum
    cm = cdiff
    step = 1
    while step < CHUNK:
        cm = jnp.maximum(cm, jnp.where(pos >= step, pltpu.roll(cm, step, 0), -jnp.inf))
        step *= 2
    c_t = cdiff.T[0:SUBLANES, :]

    m_run = m_scr[0:1, :]
    s_old, s_loc, stats = [], [], []
    for c in range(n_chunks):
        rows = slice(c * CHUNK, (c + 1) * CHUNK)
        b_c = bcum[rows]
        a_c = b_c[CHUNK - 1:CHUNK, :]
        g_c = a_c - b_c + gates[rows]
        m_loc = jnp.max(g_c, axis=0, keepdims=True)
        mm = jnp.maximum(m_run, cm[rows])
        stats.append((mm, jnp.exp(m_run - mm), jnp.exp(-(b_c + mm)), jnp.exp(g_c - m_loc)))
        m_new = jnp.maximum(a_c + m_run, m_loc)
        s_old.append(jnp.exp(a_c + m_run - m_new))
        s_loc.append(jnp.exp(m_loc - m_new))
        m_run = m_new
    m_scr[0:1, :] = m_run

    for u, (c, h) in enumerate(units):
        for i, stat in enumerate(stats[c]):
            bc_scr[u, i] = jnp.broadcast_to(stat[:, h:h + 1], (CHUNK, LANES))

    causal = (lax.broadcasted_iota(jnp.int32, (CHUNK, CHUNK), 0)
              >= lax.broadcasted_iota(jnp.int32, (CHUNK, CHUNK), 1))
    ones_rows = jnp.ones((2 * SUBLANES, CHUNK), BF16)

    for u, (c, h) in enumerate(units):
        rows = slice(c * CHUNK, (c + 1) * CHUNK)
        lanes = slice(h * ML_HEAD_DIM, (h + 1) * ML_HEAD_DIM)
        k_c = k_scr[rows, lanes]
        s_qk = lax.dot_general(q_scr[rows, lanes], k_c.astype(BF16), (((1,), (1,)), ((), ())),
                               preferred_element_type=F32)
        dw = jnp.where(causal, jnp.exp(c_t[h:h + 1, rows] - bc_scr[u, 0][:, :CHUNK]), 0.0)
        qkm_scr[u] = (s_qk * dw).astype(BF16)
        wk = (bc_scr[u, 3] * k_c).astype(BF16)
        lhs = jnp.concatenate([vt_ref[0, c, lanes, :], ones_rows], axis=0)
        res = jnp.dot(lhs, wk, preferred_element_type=F32)
        cl_scr[u] = res[:ML_HEAD_DIM]
        nl_scr[u, 0:1, :] = res[ML_HEAD_DIM:ML_HEAD_DIM + 1]

    for h in range(ML_HEADS):
        c_run = c_scr[h]
        n_run = n_scr[h:h + 1, :]
        for c in range(n_chunks):
            u = c * ML_HEADS + h
            cp_scr[u, 0:ML_HEAD_DIM, :] = c_run.astype(BF16)
            cp_scr[u, ML_HEAD_DIM:2 * ML_HEAD_DIM, :] = jnp.broadcast_to(
                n_run, (ML_HEAD_DIM, ML_HEAD_DIM)).astype(BF16)
            so = s_old[c][:, h:h + 1]
            sl = s_loc[c][:, h:h + 1]
            c_run = so * c_run + sl * cl_scr[u]
            n_run = so * n_run + sl * nl_scr[u, 0:1, :]
        c_scr[h] = c_run
        n_scr[h:h + 1, :] = n_run

    ones_cols = jnp.ones((CHUNK, ML_HEAD_DIM), BF16)
    for u, (c, h) in enumerate(units):
        rows = slice(c * CHUNK, (c + 1) * CHUNK)
        lanes = slice(h * ML_HEAD_DIM, (h + 1) * ML_HEAD_DIM)
        qc = lax.dot_general(q_scr[rows, lanes], cp_scr[u], (((1,), (1,)), ((), ())),
                             preferred_element_type=F32)
        kv = jnp.dot(qkm_scr[u], jnp.concatenate([v_ref[0, rows, lanes], ones_cols], axis=1),
                     preferred_element_type=F32)
        inter_w = bc_scr[u, 1]
        den = inter_w * qc[:, ML_HEAD_DIM:] + kv[:, ML_HEAD_DIM:]
        num = inter_w * qc[:, :ML_HEAD_DIM] + kv[:, :ML_HEAD_DIM]
        hh = num * (1.0 / jnp.maximum(jnp.abs(den), bc_scr[u, 2]))
        hg = _sigmoid(o_ref[0, rows, lanes].astype(F32)) * hh
        hg_scr[u] = hg
        h2_scr[u] = (hg * hg).astype(BF16)

    ones_sq = jnp.ones((ML_HEAD_DIM, ML_HEAD_DIM), BF16)
    for u, (c, h) in enumerate(units):
        rows = slice(c * CHUNK, (c + 1) * CHUNK)
        lanes = slice(h * ML_HEAD_DIM, (h + 1) * ML_HEAD_DIM)
        ss = jnp.dot(h2_scr[u], ones_sq, preferred_element_type=F32)
        hn = hg_scr[u] * lax.rsqrt(ss * (1.0 / ML_HEAD_DIM) + EPS) * ghead_ref[:, lanes]
        out_ref[0, rows, lanes] = (hn * _silu(z_ref[0, rows, lanes].astype(F32))).astype(BF16)


def _mlstm(ml, vt, gates, conv_w, conv_b, gbias, g_head):
    bsz, s, _ = ml.shape
    t = min(ML_T, s)
    w2 = 2 * ML_WIDTH
    n_units = (t // CHUNK) * ML_HEADS
    return pl.pallas_call(
        _mlstm_kernel,
        grid=(bsz, s // t),
        in_specs=[pl.BlockSpec((1, t, w2), lambda b, j: (b, j, 0)),
                  pl.BlockSpec((1, t, ML_WIDTH), lambda b, j: (b, j, 2)),
                  pl.BlockSpec((1, t // CHUNK, ML_WIDTH, CHUNK), lambda b, j: (b, j, 0, 0)),
                  pl.BlockSpec((1, t, ML_WIDTH), lambda b, j: (b, j, 3)),
                  pl.BlockSpec((1, t, ML_WIDTH), lambda b, j: (b, j, 4)),
                  pl.BlockSpec((1, t, LANES), lambda b, j: (b, j, 0)),
                  pl.BlockSpec((CONV_K, w2), lambda b, j: (0, 0)),
                  pl.BlockSpec((1, w2), lambda b, j: (0, 0)),
                  pl.BlockSpec((1, LANES), lambda b, j: (0, 0)),
                  pl.BlockSpec((1, ML_WIDTH), lambda b, j: (0, 0))],
        out_specs=pl.BlockSpec((1, t, ML_WIDTH), lambda b, j: (b, j, 0)),
        out_shape=jax.ShapeDtypeStruct((bsz, s, ML_WIDTH), BF16),
        scratch_shapes=[pltpu.VMEM((t + 2 * SUBLANES, w2), F32),
                        pltpu.VMEM((t, ML_WIDTH), BF16),
                        pltpu.VMEM((t, ML_WIDTH), F32),
                        pltpu.VMEM((n_units, 4, CHUNK, LANES), F32),
                        pltpu.VMEM((n_units, CHUNK, CHUNK), BF16),
                        pltpu.VMEM((n_units, ML_HEAD_DIM, ML_HEAD_DIM), F32),
                        pltpu.VMEM((n_units, SUBLANES, ML_HEAD_DIM), F32),
                        pltpu.VMEM((n_units, 2 * ML_HEAD_DIM, ML_HEAD_DIM), BF16),
                        pltpu.VMEM((n_units, CHUNK, ML_HEAD_DIM), F32),
                        pltpu.VMEM((n_units, CHUNK, ML_HEAD_DIM), BF16),
                        pltpu.VMEM((ML_HEADS, ML_HEAD_DIM, ML_HEAD_DIM), F32),
                        pltpu.VMEM((SUBLANES, ML_HEAD_DIM), F32),
                        pltpu.VMEM((SUBLANES, LANES), F32)],
        compiler_params=pltpu.CompilerParams(
            dimension_semantics=("parallel", "arbitrary"), vmem_limit_bytes=VMEM_LIMIT),
        name="mlstm",
    )(ml, ml, vt, ml, ml, gates, conv_w, conv_b.reshape(1, w2), gbias, g_head.reshape(1, ML_WIDTH))


def _sb_sweeps(ds, first, qt0, chains, q_ref, k_ref, v_ref, cum_ref, z_scr, n_scr, a_scr, s_scr,
               acc_scr, r_scr):
    n_chain = len(chains)
    items = [(i * n_chain + c, d, first and i == 0, g, p)
             for i, d in enumerate(ds) for c, (g, p) in enumerate(chains)]
    if first:
        strict = ((lax.broadcasted_iota(jnp.int32, (SB_TQ, 2 * SB_TK), 1) & (SB_TK - 1))
                  < lax.broadcasted_iota(jnp.int32, (SB_TQ, 2 * SB_TK), 0))
    lane_a = lax.broadcasted_iota(jnp.int32, (SB_TQ, LANES), 1) < SB_HEAD_DIM

    def key_rows(g, d):
        return pl.ds(jnp.maximum(qt0 + g - d, 0) * (2 * SB_TK), 2 * SB_TK)

    def item_rows(slot):
        return slice(slot * SB_TQ, (slot + 1) * SB_TQ)

    for slot, d, diag, g, p in items:
        q2 = q_ref[0, p, g * SB_TQ:(g + 1) * SB_TQ, :]
        z = lax.dot_general(q2, k_ref[0, p, key_rows(g, d), :], (((1,), (1,)), ((), ())),
                            preferred_element_type=F32)
        nlk = jnp.log(1.0 + jnp.exp2(jnp.minimum(z, SB_ZMAX))) * LOG2E
        if diag:
            nlk = jnp.where(strict, nlk, 0.0)
        z_scr[slot] = z
        n_scr[item_rows(slot), :] = nlk.astype(BF16)

    m_all = jnp.dot(n_scr[0:len(items) * SB_TQ, :], cum_ref[...], preferred_element_type=F32)
    for slot, d, diag, g, p in items:
        m = m_all[item_rows(slot)]
        a = jnp.exp2(z_scr[slot] - m)
        if diag:
            a = jnp.where(strict, a, 0.0)
        a_scr[slot] = a.astype(BF16)
        s_scr[slot] = jnp.where(lane_a, m[:, 0:1], m[:, SB_TK:SB_TK + 1])

    rmin = None
    for c, (g, p) in enumerate(chains):
        acc, r = (None, None) if first else (acc_scr[c], r_scr[c])
        for i, d in enumerate(ds):
            slot = i * n_chain + c
            pv = jnp.dot(a_scr[slot], v_ref[0, p, key_rows(g, d), :], preferred_element_type=F32)
            if first and i == 0:
                acc, r = pv, s_scr[slot]
            else:
                r = jnp.where(qt0 + g - d >= 0, r, SB_DEAD)
                acc = acc + jnp.exp2(-r) * pv
                r = r + s_scr[slot]
        acc_scr[c] = acc
        r_scr[c] = r
        rmin = r if rmin is None else jnp.minimum(rmin, r)
    return jnp.min(rmin)


def _sb_kernel(q_ref, k_ref, v_ref, z_ref, cum_ref, o_ref, z_scr, n_scr, a_scr, s_scr, acc_scr, r_scr):
    n_sub = q_ref.shape[2] // SB_TQ
    qt0 = pl.program_id(1) * n_sub
    chains = [(g, p) for g in range(n_sub) for p in range(SB_PAIRS)]
    sweeps = functools.partial(_sb_sweeps, qt0=qt0, chains=chains, q_ref=q_ref, k_ref=k_ref,
                               v_ref=v_ref, cum_ref=cum_ref, z_scr=z_scr, n_scr=n_scr, a_scr=a_scr,
                               s_scr=s_scr, acc_scr=acc_scr, r_scr=r_scr)

    def cond(carry):
        d, rm = carry
        return jnp.logical_and(qt0 + (n_sub - 1) - d >= 0, rm < SB_SKIP)

    def body(carry):
        d, _ = carry
        return d + 1, sweeps([d], False)

    rm = sweeps(list(range(SB_MIN_SWEEPS)), True)
    lax.while_loop(cond, body, (jnp.int32(SB_MIN_SWEEPS), rm))

    for c, (g, p) in enumerate(chains):
        zc = z_ref[0, p, g * SB_TQ:(g + 1) * SB_TQ, :].astype(F32)
        o_ref[0, p, g * SB_TQ:(g + 1) * SB_TQ, :] = (acc_scr[c] * _silu(zc)).astype(BF16)


def _stick_breaking(sbq, sbk, sbv, sbz):
    bsz, pairs, s, _ = sbq.shape
    rows = min(SB_ROWS, s)
    j = lax.broadcasted_iota(jnp.int32, (2 * SB_TK, 2 * SB_TK), 0)
    c = lax.broadcasted_iota(jnp.int32, (2 * SB_TK, 2 * SB_TK), 1)
    cum = jnp.where((j >= c) & ((j // SB_TK) == (c // SB_TK)), 1.0, 0.0).astype(BF16)
    q_spec = pl.BlockSpec((1, pairs, rows, LANES), lambda b, i: (b, 0, i, 0))
    kv_spec = pl.BlockSpec((1, pairs, 2 * s, LANES), lambda b, i: (b, 0, 0, 0))
    n_chain = (rows // SB_TQ) * pairs
    n_item = SB_MIN_SWEEPS * n_chain
    return pl.pallas_call(
        _sb_kernel,
        grid=(bsz, s // rows),
        in_specs=[q_spec, kv_spec, kv_spec, q_spec,
                  pl.BlockSpec((2 * SB_TK, 2 * SB_TK), lambda b, i: (0, 0))],
        out_specs=q_spec,
        out_shape=jax.ShapeDtypeStruct((bsz, pairs, s, LANES), BF16),
        scratch_shapes=[pltpu.VMEM((n_item, SB_TQ, 2 * SB_TK), F32),
                        pltpu.VMEM((n_item * SB_TQ, 2 * SB_TK), BF16),
                        pltpu.VMEM((n_item, SB_TQ, 2 * SB_TK), BF16),
                        pltpu.VMEM((n_item, SB_TQ, LANES), F32),
                        pltpu.VMEM((n_chain, SB_TQ, LANES), F32),
                        pltpu.VMEM((n_chain, SB_TQ, LANES), F32)],
        compiler_params=pltpu.CompilerParams(
            dimension_semantics=("parallel", "arbitrary"), vmem_limit_bytes=VMEM_LIMIT),
        name="stickbreak",
    )(sbq, sbk, sbv, sbz, cum)


def _outproj_kernel(hml_ref, hsb_ref, x_ref, gate_ref, gpost_ref, wml_ref, wsb_ref, o_ref):
    hsb = jnp.concatenate([hsb_ref[0, p] for p in range(SB_PAIRS)], axis=1)
    y = (jnp.dot(hml_ref[0], wml_ref[...], preferred_element_type=F32)
         + jnp.dot(hsb, wsb_ref[...], preferred_element_type=F32))
    r = lax.rsqrt(jnp.mean(y * y, axis=-1, keepdims=True) + EPS)
    o_ref[0] = x_ref[0] + gate_ref[0] * ((y * r) * gpost_ref[...])


def _outproj(hml, hsb, x, mod3, g_post, w_oml, w_osb):
    bsz, s, d = x.shape
    tm = min(OUT_TM, s)
    return pl.pallas_call(
        _outproj_kernel,
        grid=(bsz, s // tm),
        in_specs=[pl.BlockSpec((1, tm, ML_WIDTH), lambda b, i: (b, i, 0)),
                  pl.BlockSpec((1, SB_PAIRS, tm, LANES), lambda b, i: (b, 0, i, 0)),
                  pl.BlockSpec((1, tm, d), lambda b, i: (b, i, 0)),
                  pl.BlockSpec((1, 1, d), lambda b, i: (b, 0, 2)),
                  pl.BlockSpec((1, d), lambda b, i: (0, 0)),
                  pl.BlockSpec((ML_WIDTH, d), lambda b, i: (0, 0)),
                  pl.BlockSpec((SB_WIDTH, d), lambda b, i: (0, 0))],
        out_specs=pl.BlockSpec((1, tm, d), lambda b, i: (b, i, 0)),
        out_shape=jax.ShapeDtypeStruct((bsz, s, d), F32),
        compiler_params=pltpu.CompilerParams(
            dimension_semantics=("parallel", "parallel"), vmem_limit_bytes=VMEM_LIMIT),
        name="outproj",
    )(hml, hsb, x, mod3, g_post.reshape(1, d), w_oml, w_osb)


def kernel(x, c, w_ada, b_ada, g_pre, w_in, b_igate, b_fgate, conv_w, conv_b, g_ml_head, w_out, g_post):
    bsz, s, d = x.shape
    o_gate = 2 * ML_WIDTH + 3 * ML_WIDTH
    o_sb = o_gate + 2 * ML_HEADS
    w_ml = w_in[:, :o_gate].astype(BF16)
    w_g = jnp.pad(w_in[:, o_gate:o_sb], ((0, 0), (0, LANES - 2 * ML_HEADS))).astype(BF16)
    q_scale = jnp.where(jnp.arange(4 * SB_WIDTH) < SB_WIDTH, LOG2E * SB_HEAD_DIM ** -0.5, 1.0)
    w_sb = (w_in[:, o_sb:] * q_scale).astype(BF16)
    gbias = jnp.pad(jnp.concatenate([b_igate, b_fgate]), (0, LANES - 2 * ML_HEADS)).reshape(1, LANES)
    w_o = w_out.astype(BF16)

    mod3 = _adaln(c, w_ada, b_ada).reshape(bsz, 1, 3 * d)
    w_vt = w_in[:, 2 * ML_WIDTH:3 * ML_WIDTH].T.astype(BF16)
    ml, vt, sbq, sbk, sbv, sbz, gates = _inproj(x, mod3, g_pre, w_ml, w_vt, w_sb, w_g)
    hml = _mlstm(ml, vt, gates, conv_w, conv_b, gbias, g_ml_head)
    hsb = _stick_breaking(sbq, sbk, sbv, sbz)
    return _outproj(hml, hsb, x, mod3, g_post, w_o[:ML_WIDTH], w_o[ML_WIDTH:])
```

```python
import functools

import jax
import jax.numpy as jnp
from jax import lax
from jax.experimental import pallas as pl
from jax.experimental.pallas import tpu as pltpu

F32 = jnp.float32
BF16 = jnp.bfloat16
HIGHEST = lax.Precision.HIGHEST

EPS = 1e-6
ML_HEADS = 4
ML_HEAD_DIM = 128
ML_WIDTH = ML_HEADS * ML_HEAD_DIM
SB_HEADS = 8
SB_HEAD_DIM = 64
SB_WIDTH = SB_HEADS * SB_HEAD_DIM
SB_PAIRS = SB_HEADS // 2
CONV_K = 4
CHUNK = 64
LANES = 128
SUBLANES = 8

IN_TM = 512
ML_T = 256
SB_TQ = 128
SB_TK = 128
SB_ROWS = 256
OUT_TM = 1024
LOG2E = 1.4426950408889634
SB_SKIP = 128.0
SB_DEAD = 1e30
SB_MIN_SWEEPS = 3
SB_ZMAX = 126.0
VMEM_LIMIT = 56 * 1024 * 1024


def _sigmoid(x):
    return 0.5 + 0.5 * jnp.tanh(0.5 * x)


def _silu(x):
    hx = 0.5 * x
    return hx + hx * jnp.tanh(hx)


def _softplus(x):
    return jnp.maximum(x, 0.0) + jnp.log(1.0 + jnp.exp(-jnp.abs(x)))


def _adaln_kernel(c_ref, w_ref, b_ref, o_ref):
    c = c_ref[...]
    s = _silu(c)
    o_ref[...] = jnp.dot(s, w_ref[...], preferred_element_type=F32, precision=HIGHEST) + b_ref[...]


def _adaln(c, w_ada, b_ada):
    bsz, d = c.shape
    n = w_ada.shape[1]
    return pl.pallas_call(
        _adaln_kernel,
        grid=(n // d,),
        in_specs=[pl.BlockSpec((bsz, d), lambda j: (0, 0)),
                  pl.BlockSpec((d, d), lambda j: (0, j)),
                  pl.BlockSpec((1, d), lambda j: (0, j))],
        out_specs=pl.BlockSpec((bsz, d), lambda j: (0, j)),
        out_shape=jax.ShapeDtypeStruct((bsz, n), F32),
        name="adaln",
    )(c, w_ada, b_ada.reshape(1, n))


def _inproj_kernel(x_ref, shift_ref, scale_ref, gpre_ref, wml_ref, wkt_ref, wsb_ref, wg_ref,
                   ml_ref, sbq_ref, sbk_ref, sbv_ref, sbz_ref, gates_ref, h_scr):
    x = x_ref[0]
    r = lax.rsqrt(jnp.mean(x * x, axis=-1, keepdims=True) + EPS)
    a = gpre_ref[...] * (1.0 + scale_ref[0])
    h_scr[...] = ((x * r) * a + shift_ref[0]).astype(BF16)

    n_ml = wml_ref.shape[1]
    for n in range(n_ml // 512):
        cols = slice(n * 512, (n + 1) * 512)
        ml_ref[0, :, cols] = jnp.dot(h_scr[...], wml_ref[:, cols],
                                     preferred_element_type=F32).astype(BF16)
    tm = x.shape[0]
    kt = lax.dot_general(wkt_ref[...], h_scr[...], (((1,), (1,)), ((), ())),
                         preferred_element_type=F32).astype(BF16)
    d_a = lax.broadcasted_iota(jnp.int32, (LANES, tm), 0) < SB_HEAD_DIM
    for p in range(SB_PAIRS):
        blk = kt[p * LANES:(p + 1) * LANES, :]
        zero = jnp.zeros_like(blk)
        only_a = jnp.where(d_a, blk, zero)
        only_b = jnp.where(d_a, zero, blk)
        for j in range(tm // SB_TK):
            cols = slice(j * SB_TK, (j + 1) * SB_TK)
            sbk_ref[0, p, j, :, 0:SB_TK] = only_a[:, cols]
            sbk_ref[0, p, j, :, SB_TK:2 * SB_TK] = only_b[:, cols]
    half = lax.broadcasted_iota(jnp.int32, (tm, LANES), 1) < SB_HEAD_DIM
    for n, dst in enumerate((sbq_ref, sbv_ref, sbz_ref)):
        cols = slice(n * SB_WIDTH, (n + 1) * SB_WIDTH)
        res = jnp.dot(h_scr[...], wsb_ref[:, cols], preferred_element_type=F32).astype(BF16)
        for p in range(SB_PAIRS):
            blk = res[:, p * LANES:(p + 1) * LANES]
            if dst is sbv_ref:
                zero = jnp.zeros_like(blk)
                only_a = jnp.where(half, blk, zero)
                only_b = jnp.where(half, zero, blk)
                for j in range(tm // SB_TK):
                    rows = slice(j * SB_TK, (j + 1) * SB_TK)
                    dst[0, p, 2 * j * SB_TK:(2 * j + 1) * SB_TK, :] = only_a[rows]
                    dst[0, p, (2 * j + 1) * SB_TK:(2 * j + 2) * SB_TK, :] = only_b[rows]
            else:
                dst[0, p] = blk
    gates_ref[0] = jnp.dot(h_scr[...], wg_ref[...], preferred_element_type=F32)


def _inproj(x, mod3, g_pre, w_ml, w_kt, w_sb, w_g):
    bsz, s, d = x.shape
    tm = min(IN_TM, s)
    n_ml = w_ml.shape[1]
    sb_shape = jax.ShapeDtypeStruct((bsz, SB_PAIRS, s, LANES), BF16)
    sb_spec = pl.BlockSpec((1, SB_PAIRS, tm, LANES), lambda b, i: (b, 0, i, 0))
    k_shape = jax.ShapeDtypeStruct((bsz, SB_PAIRS, s // SB_TK, LANES, 2 * SB_TK), BF16)
    k_spec = pl.BlockSpec((1, SB_PAIRS, tm // SB_TK, LANES, 2 * SB_TK), lambda b, i: (b, 0, i, 0, 0))
    v_shape = jax.ShapeDtypeStruct((bsz, SB_PAIRS, 2 * s, LANES), BF16)
    v_spec = pl.BlockSpec((1, SB_PAIRS, 2 * tm, LANES), lambda b, i: (b, 0, i, 0))
    return pl.pallas_call(
        _inproj_kernel,
        grid=(bsz, s // tm),
        in_specs=[pl.BlockSpec((1, tm, d), lambda b, i: (b, i, 0)),
                  pl.BlockSpec((1, 1, d), lambda b, i: (b, 0, 0)),
                  pl.BlockSpec((1, 1, d), lambda b, i: (b, 0, 1)),
                  pl.BlockSpec((1, d), lambda b, i: (0, 0)),
                  pl.BlockSpec((d, n_ml), lambda b, i: (0, 0)),
                  pl.BlockSpec((SB_WIDTH, d), lambda b, i: (0, 0)),
                  pl.BlockSpec((d, 3 * SB_WIDTH), lambda b, i: (0, 0)),
                  pl.BlockSpec((d, LANES), lambda b, i: (0, 0))],
        out_specs=[pl.BlockSpec((1, tm, n_ml), lambda b, i: (b, i, 0)),
                   sb_spec, k_spec, v_spec, sb_spec,
                   pl.BlockSpec((1, tm, LANES), lambda b, i: (b, i, 0))],
        out_shape=[jax.ShapeDtypeStruct((bsz, s, n_ml), BF16),
                   sb_shape, k_shape, v_shape, sb_shape,
                   jax.ShapeDtypeStruct((bsz, s, LANES), F32)],
        scratch_shapes=[pltpu.VMEM((tm, d), BF16)],
        compiler_params=pltpu.CompilerParams(
            dimension_semantics=("parallel", "parallel"), vmem_limit_bytes=VMEM_LIMIT),
        name="inproj",
    )(x, mod3, mod3, g_pre.reshape(1, d), w_ml, w_kt, w_sb, w_g)


def _mlstm_kernel(u_ref, v_ref, o_ref, z_ref, g_ref, convw_ref, convb_ref, gbias_ref,
                  ghead_ref, out_ref, ubuf, q_scr, k_scr, bc_scr, qkm_scr, cl_scr, nl_scr, cp_scr,
                  hg_scr, h2_scr, c_scr, n_scr, m_scr):
    t = u_ref.shape[1]
    n_chunks = t // CHUNK
    units = [(c, h) for c in range(n_chunks) for h in range(ML_HEADS)]
    j = pl.program_id(1)

    @pl.when(j == 0)
    def _():
        ubuf[0:SUBLANES, :] = jnp.zeros((SUBLANES, ubuf.shape[1]), F32)
        c_scr[...] = jnp.zeros_like(c_scr)
        n_scr[...] = jnp.zeros_like(n_scr)
        m_scr[...] = jnp.zeros_like(m_scr)

    @pl.when(j > 0)
    def _():
        ubuf[0:SUBLANES, :] = ubuf[t:t + SUBLANES, :]

    ubuf[SUBLANES:SUBLANES + t, :] = u_ref[0].astype(F32)

    conv = convb_ref[...]
    for i in range(CONV_K):
        off = SUBLANES - (CONV_K - 1) + i
        conv = conv + convw_ref[i:i + 1, :] * ubuf[off:off + t, :]
    qk = _silu(conv)
    q_scr[...] = qk[:, :ML_WIDTH].astype(BF16)
    k_scr[...] = qk[:, ML_WIDTH:] * (ML_HEAD_DIM ** -0.5)

    gates = g_ref[0] + gbias_ref[...]
    pos = lax.broadcasted_iota(jnp.int32, (t, LANES), 0) & (CHUNK - 1)
    bcum = -_softplus(-gates)
    step = 1
    while step < CHUNK:
        bcum = bcum + jnp.where(pos >= step, pltpu.roll(bcum, step, 0), 0.0)
        step *= 2
    bcum = pltpu.roll(bcum, LANES - ML_HEADS, 1)
    cdiff = gates - bcum
    cm = cdiff
    step = 1
    while step < CHUNK:
        cm = jnp.maximum(cm, jnp.where(pos >= step, pltpu.roll(cm, step, 0), -jnp.inf))
        step *= 2
    c_t = cdiff.T[0:SUBLANES, :]

    m_run = m_scr[0:1, :]
    s_old, s_loc, stats = [], [], []
    for c in range(n_chunks):
        rows = slice(c * CHUNK, (c + 1) * CHUNK)
        b_c = bcum[rows]
        a_c = b_c[CHUNK - 1:CHUNK, :]
        g_c = a_c - b_c + gates[rows]
        m_loc = jnp.max(g_c, axis=0, keepdims=True)
        mm = jnp.maximum(m_run, cm[rows])
        stats.append((mm, jnp.exp(m_run - mm), jnp.exp(-(b_c + mm)), jnp.exp(g_c - m_loc)))
        m_new = jnp.maximum(a_c + m_run, m_loc)
        s_old.append(jnp.exp(a_c + m_run - m_new))
        s_loc.append(jnp.exp(m_loc - m_new))
        m_run = m_new
    m_scr[0:1, :] = m_run

    for u, (c, h) in enumerate(units):
        for i, stat in enumerate(stats[c]):
            bc_scr[u, i] = jnp.broadcast_to(stat[:, h:h + 1], (CHUNK, LANES))

    causal = (lax.broadcasted_iota(jnp.int32, (CHUNK, CHUNK), 0)
              >= lax.broadcasted_iota(jnp.int32, (CHUNK, CHUNK), 1))
    ones_rows = jnp.ones((2 * SUBLANES, CHUNK), BF16)

    for u, (c, h) in enumerate(units):
        rows = slice(c * CHUNK, (c + 1) * CHUNK)
        lanes = slice(h * ML_HEAD_DIM, (h + 1) * ML_HEAD_DIM)
        k_c = k_scr[rows, lanes]
        s_qk = lax.dot_general(q_scr[rows, lanes], k_c.astype(BF16), (((1,), (1,)), ((), ())),
                               preferred_element_type=F32)
        dw = jnp.where(causal, jnp.exp(c_t[h:h + 1, rows] - bc_scr[u, 0][:, :CHUNK]), 0.0)
        qkm_scr[u] = (s_qk * dw).astype(BF16)
        wk = (bc_scr[u, 3] * k_c).astype(BF16)
        cl_scr[u] = lax.dot_general(v_ref[0, rows, lanes], wk, (((0,), (0,)), ((), ())),
                                    preferred_element_type=F32)
        nl_scr[u] = jnp.dot(ones_rows, wk, preferred_element_type=F32)[:SUBLANES]

    for h in range(ML_HEADS):
        c_run = c_scr[h]
        n_run = n_scr[h:h + 1, :]
        for c in range(n_chunks):
            u = c * ML_HEADS + h
            cp_scr[u, 0:ML_HEAD_DIM, :] = c_run.astype(BF16)
            cp_scr[u, ML_HEAD_DIM:2 * ML_HEAD_DIM, :] = jnp.broadcast_to(
                n_run, (ML_HEAD_DIM, ML_HEAD_DIM)).astype(BF16)
            so = s_old[c][:, h:h + 1]
            sl = s_loc[c][:, h:h + 1]
            c_run = so * c_run + sl * cl_scr[u]
            n_run = so * n_run + sl * nl_scr[u, 0:1, :]
        c_scr[h] = c_run
        n_scr[h:h + 1, :] = n_run

    ones_cols = jnp.ones((CHUNK, ML_HEAD_DIM), BF16)
    for u, (c, h) in enumerate(units):
        rows = slice(c * CHUNK, (c + 1) * CHUNK)
        lanes = slice(h * ML_HEAD_DIM, (h + 1) * ML_HEAD_DIM)
        qc = lax.dot_general(q_scr[rows, lanes], cp_scr[u], (((1,), (1,)), ((), ())),
                             preferred_element_type=F32)
        kv = jnp.dot(qkm_scr[u], jnp.concatenate([v_ref[0, rows, lanes], ones_cols], axis=1),
                     preferred_element_type=F32)
        inter_w = bc_scr[u, 1]
        den = inter_w * qc[:, ML_HEAD_DIM:] + kv[:, ML_HEAD_DIM:]
        num = inter_w * qc[:, :ML_HEAD_DIM] + kv[:, :ML_HEAD_DIM]
        hh = num * (1.0 / jnp.maximum(jnp.abs(den), bc_scr[u, 2]))
        hg = _sigmoid(o_ref[0, rows, lanes].astype(F32)) * hh
        hg_scr[u] = hg
        h2_scr[u] = (hg * hg).astype(BF16)

    ones_sq = jnp.ones((ML_HEAD_DIM, ML_HEAD_DIM), BF16)
    for u, (c, h) in enumerate(units):
        rows = slice(c * CHUNK, (c + 1) * CHUNK)
        lanes = slice(h * ML_HEAD_DIM, (h + 1) * ML_HEAD_DIM)
        ss = jnp.dot(h2_scr[u], ones_sq, preferred_element_type=F32)
        hn = hg_scr[u] * lax.rsqrt(ss * (1.0 / ML_HEAD_DIM) + EPS) * ghead_ref[:, lanes]
        out_ref[0, rows, lanes] = (hn * _silu(z_ref[0, rows, lanes].astype(F32))).astype(BF16)


def _mlstm(ml, gates, conv_w, conv_b, gbias, g_head):
    bsz, s, _ = ml.shape
    t = min(ML_T, s)
    w2 = 2 * ML_WIDTH
    n_units = (t // CHUNK) * ML_HEADS
    return pl.pallas_call(
        _mlstm_kernel,
        grid=(bsz, s // t),
        in_specs=[pl.BlockSpec((1, t, w2), lambda b, j: (b, j, 0)),
                  pl.BlockSpec((1, t, ML_WIDTH), lambda b, j: (b, j, 2)),
                  pl.BlockSpec((1, t, ML_WIDTH), lambda b, j: (b, j, 3)),
                  pl.BlockSpec((1, t, ML_WIDTH), lambda b, j: (b, j, 4)),
                  pl.BlockSpec((1, t, LANES), lambda b, j: (b, j, 0)),
                  pl.BlockSpec((CONV_K, w2), lambda b, j: (0, 0)),
                  pl.BlockSpec((1, w2), lambda b, j: (0, 0)),
                  pl.BlockSpec((1, LANES), lambda b, j: (0, 0)),
                  pl.BlockSpec((1, ML_WIDTH), lambda b, j: (0, 0))],
        out_specs=pl.BlockSpec((1, t, ML_WIDTH), lambda b, j: (b, j, 0)),
        out_shape=jax.ShapeDtypeStruct((bsz, s, ML_WIDTH), BF16),
        scratch_shapes=[pltpu.VMEM((t + 2 * SUBLANES, w2), F32),
                        pltpu.VMEM((t, ML_WIDTH), BF16),
                        pltpu.VMEM((t, ML_WIDTH), F32),
                        pltpu.VMEM((n_units, 4, CHUNK, LANES), F32),
                        pltpu.VMEM((n_units, CHUNK, CHUNK), BF16),
                        pltpu.VMEM((n_units, ML_HEAD_DIM, ML_HEAD_DIM), F32),
                        pltpu.VMEM((n_units, SUBLANES, ML_HEAD_DIM), F32),
                        pltpu.VMEM((n_units, 2 * ML_HEAD_DIM, ML_HEAD_DIM), BF16),
                        pltpu.VMEM((n_units, CHUNK, ML_HEAD_DIM), F32),
                        pltpu.VMEM((n_units, CHUNK, ML_HEAD_DIM), BF16),
                        pltpu.VMEM((ML_HEADS, ML_HEAD_DIM, ML_HEAD_DIM), F32),
                        pltpu.VMEM((SUBLANES, ML_HEAD_DIM), F32),
                        pltpu.VMEM((SUBLANES, LANES), F32)],
        compiler_params=pltpu.CompilerParams(
            dimension_semantics=("parallel", "arbitrary"), vmem_limit_bytes=VMEM_LIMIT),
        name="mlstm",
    )(ml, ml, ml, ml, gates, conv_w, conv_b.reshape(1, w2), gbias, g_head.reshape(1, ML_WIDTH))


def _sb_sweeps(ds, first, qt0, chains, q_ref, k_ref, v_ref, cum_ref, z_scr, n_scr, a_scr, s_scr,
               acc_scr, r_scr):
    n_chain = len(chains)
    items = [(i * n_chain + c, d, first and i == 0, g, p)
             for i, d in enumerate(ds) for c, (g, p) in enumerate(chains)]
    if first:
        strict = ((lax.broadcasted_iota(jnp.int32, (SB_TQ, 2 * SB_TK), 1) & (SB_TK - 1))
                  < lax.broadcasted_iota(jnp.int32, (SB_TQ, 2 * SB_TK), 0))
    lane_a = lax.broadcasted_iota(jnp.int32, (SB_TQ, LANES), 1) < SB_HEAD_DIM

    def key_rows(g, d):
        return pl.ds(jnp.maximum(qt0 + g - d, 0) * (2 * SB_TK), 2 * SB_TK)

    def item_rows(slot):
        return slice(slot * SB_TQ, (slot + 1) * SB_TQ)

    for slot, d, diag, g, p in items:
        q2 = q_ref[0, p, g * SB_TQ:(g + 1) * SB_TQ, :]
        z = jnp.dot(q2, k_ref[0, p, jnp.maximum(qt0 + g - d, 0)],
                    preferred_element_type=F32)
        nlk = jnp.log(1.0 + jnp.exp2(jnp.minimum(z, SB_ZMAX))) * LOG2E
        if diag:
            nlk = jnp.where(strict, nlk, 0.0)
        z_scr[slot] = z
        n_scr[item_rows(slot), :] = nlk.astype(BF16)

    m_all = jnp.dot(n_scr[0:len(items) * SB_TQ, :], cum_ref[...], preferred_element_type=F32)
    for slot, d, diag, g, p in items:
        m = m_all[item_rows(slot)]
        a = jnp.exp2(z_scr[slot] - m)
        if diag:
            a = jnp.where(strict, a, 0.0)
        a_scr[slot] = a.astype(BF16)
        s_scr[slot] = jnp.where(lane_a, m[:, 0:1], m[:, SB_TK:SB_TK + 1])

    rmin = None
    for c, (g, p) in enumerate(chains):
        acc, r = (None, None) if first else (acc_scr[c], r_scr[c])
        for i, d in enumerate(ds):
            slot = i * n_chain + c
            pv = jnp.dot(a_scr[slot], v_ref[0, p, key_rows(g, d), :], preferred_element_type=F32)
            if first and i == 0:
                acc, r = pv, s_scr[slot]
            else:
                r = jnp.where(qt0 + g - d >= 0, r, SB_DEAD)
                acc = acc + jnp.exp2(-r) * pv
                r = r + s_scr[slot]
        acc_scr[c] = acc
        r_scr[c] = r
        rmin = r if rmin is None else jnp.minimum(rmin, r)
    return jnp.min(rmin)


def _sb_kernel(q_ref, k_ref, v_ref, z_ref, cum_ref, o_ref, z_scr, n_scr, a_scr, s_scr, acc_scr, r_scr):
    n_sub = q_ref.shape[2] // SB_TQ
    qt0 = pl.program_id(1) * n_sub
    chains = [(g, p) for g in range(n_sub) for p in range(SB_PAIRS)]
    sweeps = functools.partial(_sb_sweeps, qt0=qt0, chains=chains, q_ref=q_ref, k_ref=k_ref,
                               v_ref=v_ref, cum_ref=cum_ref, z_scr=z_scr, n_scr=n_scr, a_scr=a_scr,
                               s_scr=s_scr, acc_scr=acc_scr, r_scr=r_scr)

    def cond(carry):
        d, rm = carry
        return jnp.logical_and(qt0 + (n_sub - 1) - d >= 0, rm < SB_SKIP)

    def body(carry):
        d, _ = carry
        return d + 1, sweeps([d], False)

    rm = sweeps(list(range(SB_MIN_SWEEPS)), True)
    lax.while_loop(cond, body, (jnp.int32(SB_MIN_SWEEPS), rm))

    for c, (g, p) in enumerate(chains):
        zc = z_ref[0, p, g * SB_TQ:(g + 1) * SB_TQ, :].astype(F32)
        o_ref[0, p, g * SB_TQ:(g + 1) * SB_TQ, :] = (acc_scr[c] * _silu(zc)).astype(BF16)


def _stick_breaking(sbq, sbk, sbv, sbz):
    bsz, pairs, s, _ = sbq.shape
    rows = min(SB_ROWS, s)
    j = lax.broadcasted_iota(jnp.int32, (2 * SB_TK, 2 * SB_TK), 0)
    c = lax.broadcasted_iota(jnp.int32, (2 * SB_TK, 2 * SB_TK), 1)
    cum = jnp.where((j >= c) & ((j // SB_TK) == (c // SB_TK)), 1.0, 0.0).astype(BF16)
    q_spec = pl.BlockSpec((1, pairs, rows, LANES), lambda b, i: (b, 0, i, 0))
    k_spec = pl.BlockSpec((1, pairs, s // SB_TK, LANES, 2 * SB_TK), lambda b, i: (b, 0, 0, 0, 0))
    v_spec = pl.BlockSpec((1, pairs, 2 * s, LANES), lambda b, i: (b, 0, 0, 0))
    n_chain = (rows // SB_TQ) * pairs
    n_item = SB_MIN_SWEEPS * n_chain
    return pl.pallas_call(
        _sb_kernel,
        grid=(bsz, s // rows),
        in_specs=[q_spec, k_spec, v_spec, q_spec,
                  pl.BlockSpec((2 * SB_TK, 2 * SB_TK), lambda b, i: (0, 0))],
        out_specs=q_spec,
        out_shape=jax.ShapeDtypeStruct((bsz, pairs, s, LANES), BF16),
        scratch_shapes=[pltpu.VMEM((n_item, SB_TQ, 2 * SB_TK), F32),
                        pltpu.VMEM((n_item * SB_TQ, 2 * SB_TK), BF16),
                        pltpu.VMEM((n_item, SB_TQ, 2 * SB_TK), BF16),
                        pltpu.VMEM((n_item, SB_TQ, LANES), F32),
                        pltpu.VMEM((n_chain, SB_TQ, LANES), F32),
                        pltpu.VMEM((n_chain, SB_TQ, LANES), F32)],
        compiler_params=pltpu.CompilerParams(
            dimension_semantics=("parallel", "arbitrary"), vmem_limit_bytes=VMEM_LIMIT),
        name="stickbreak",
    )(sbq, sbk, sbv, sbz, cum)


def _outproj_kernel(hml_ref, hsb_ref, x_ref, gate_ref, gpost_ref, wml_ref, wsb_ref, o_ref):
    hsb = jnp.concatenate([hsb_ref[0, p] for p in range(SB_PAIRS)], axis=1)
    y = (jnp.dot(hml_ref[0], wml_ref[...], preferred_element_type=F32)
         + jnp.dot(hsb, wsb_ref[...], preferred_element_type=F32))
    r = lax.rsqrt(jnp.mean(y * y, axis=-1, keepdims=True) + EPS)
    o_ref[0] = x_ref[0] + gate_ref[0] * ((y * r) * gpost_ref[...])


def _outproj(hml, hsb, x, mod3, g_post, w_oml, w_osb):
    bsz, s, d = x.shape
    tm = min(OUT_TM, s)
    return pl.pallas_call(
        _outproj_kernel,
        grid=(bsz, s // tm),
        in_specs=[pl.BlockSpec((1, tm, ML_WIDTH), lambda b, i: (b, i, 0)),
                  pl.BlockSpec((1, SB_PAIRS, tm, LANES), lambda b, i: (b, 0, i, 0)),
                  pl.BlockSpec((1, tm, d), lambda b, i: (b, i, 0)),
                  pl.BlockSpec((1, 1, d), lambda b, i: (b, 0, 2)),
                  pl.BlockSpec((1, d), lambda b, i: (0, 0)),
                  pl.BlockSpec((ML_WIDTH, d), lambda b, i: (0, 0)),
                  pl.BlockSpec((SB_WIDTH, d), lambda b, i: (0, 0))],
        out_specs=pl.BlockSpec((1, tm, d), lambda b, i: (b, i, 0)),
        out_shape=jax.ShapeDtypeStruct((bsz, s, d), F32),
        compiler_params=pltpu.CompilerParams(
            dimension_semantics=("parallel", "parallel"), vmem_limit_bytes=VMEM_LIMIT),
        name="outproj",
    )(hml, hsb, x, mod3, g_post.reshape(1, d), w_oml, w_osb)


def kernel(x, c, w_ada, b_ada, g_pre, w_in, b_igate, b_fgate, conv_w, conv_b, g_ml_head, w_out, g_post):
    bsz, s, d = x.shape
    o_gate = 2 * ML_WIDTH + 3 * ML_WIDTH
    o_sb = o_gate + 2 * ML_HEADS
    w_ml = w_in[:, :o_gate].astype(BF16)
    w_g = jnp.pad(w_in[:, o_gate:o_sb], ((0, 0), (0, LANES - 2 * ML_HEADS))).astype(BF16)
    sb_q, sb_k, sb_v, sb_z = (w_in[:, o_sb + n * SB_WIDTH:o_sb + (n + 1) * SB_WIDTH] for n in range(4))
    w_sb = jnp.concatenate([sb_q * (LOG2E * SB_HEAD_DIM ** -0.5), sb_v, sb_z], axis=1).astype(BF16)
    w_kt = sb_k.T.astype(BF16)
    gbias = jnp.pad(jnp.concatenate([b_igate, b_fgate]), (0, LANES - 2 * ML_HEADS)).reshape(1, LANES)
    w_o = w_out.astype(BF16)

    mod3 = _adaln(c, w_ada, b_ada).reshape(bsz, 1, 3 * d)
    ml, sbq, sbk, sbv, sbz, gates = _inproj(x, mod3, g_pre, w_ml, w_kt, w_sb, w_g)
    hml = _mlstm(ml, gates, conv_w, conv_b, gbias, g_ml_head)
    hsb = _stick_breaking(sbq, sbk, sbv, sbz)
    return _outproj(hml, hsb, x, mod3, g_post, w_o[:ML_WIDTH], w_o[ML_WIDTH:])
```

```python
import functools

import jax
import jax.numpy as jnp
from jax import lax
from jax.experimental import pallas as pl
from jax.experimental.pallas import tpu as pltpu

F32 = jnp.float32
BF16 = jnp.bfloat16
HIGHEST = lax.Precision.HIGHEST

EPS = 1e-6
ML_HEADS = 4
ML_HEAD_DIM = 128
ML_WIDTH = ML_HEADS * ML_HEAD_DIM
SB_HEADS = 8
SB_HEAD_DIM = 64
SB_WIDTH = SB_HEADS * SB_HEAD_DIM
SB_PAIRS = SB_HEADS // 2
CONV_K = 4
CHUNK = 64
LANES = 128
SUBLANES = 8

IN_TM = 512
ML_T = 512
SB_TQ = 128
SB_TK = 128
SB_ROWS = 256
OUT_TM = 2048
LOG2E = 1.4426950408889634
SB_SKIP = 128.0
SB_DEAD = 1e30
SB_MIN_SWEEPS = 3
SB_ZMAX = 126.0
VMEM_LIMIT = 56 * 1024 * 1024


def _sigmoid(x):
    return 0.5 + 0.5 * jnp.tanh(0.5 * x)


def _silu(x):
    hx = 0.5 * x
    return hx + hx * jnp.tanh(hx)


def _softplus(x):
    return jnp.maximum(x, 0.0) + jnp.log(1.0 + jnp.exp(-jnp.abs(x)))


def _adaln_kernel(c_ref, w_ref, b_ref, o_ref):
    c = c_ref[...]
    s = _silu(c)
    o_ref[...] = jnp.dot(s, w_ref[...], preferred_element_type=F32, precision=HIGHEST) + b_ref[...]


def _adaln(c, w_ada, b_ada):
    bsz, d = c.shape
    n = w_ada.shape[1]
    return pl.pallas_call(
        _adaln_kernel,
        grid=(n // d,),
        in_specs=[pl.BlockSpec((bsz, d), lambda j: (0, 0)),
                  pl.BlockSpec((d, d), lambda j: (0, j)),
                  pl.BlockSpec((1, d), lambda j: (0, j))],
        out_specs=pl.BlockSpec((bsz, d), lambda j: (0, j)),
        out_shape=jax.ShapeDtypeStruct((bsz, n), F32),
        name="adaln",
    )(c, w_ada, b_ada.reshape(1, n))


def _inproj_kernel(x_ref, shift_ref, scale_ref, gpre_ref, wml_ref, wkt_ref, wsb_ref, wg_ref,
                   ml_ref, sbq_ref, sbk_ref, sbv_ref, sbz_ref, gates_ref, h_scr):
    x = x_ref[0]
    r = lax.rsqrt(jnp.mean(x * x, axis=-1, keepdims=True) + EPS)
    a = gpre_ref[...] * (1.0 + scale_ref[0])
    h_scr[...] = ((x * r) * a + shift_ref[0]).astype(BF16)

    n_ml = wml_ref.shape[1]
    for n in range(n_ml // 512):
        cols = slice(n * 512, (n + 1) * 512)
        ml_ref[0, :, cols] = jnp.dot(h_scr[...], wml_ref[:, cols],
                                     preferred_element_type=F32).astype(BF16)
    tm = x.shape[0]
    kt = lax.dot_general(wkt_ref[...], h_scr[...], (((1,), (1,)), ((), ())),
                         preferred_element_type=F32).astype(BF16)
    d_a = lax.broadcasted_iota(jnp.int32, (LANES, tm), 0) < SB_HEAD_DIM
    for p in range(SB_PAIRS):
        blk = kt[p * LANES:(p + 1) * LANES, :]
        zero = jnp.zeros_like(blk)
        only_a = jnp.where(d_a, blk, zero)
        only_b = jnp.where(d_a, zero, blk)
        for j in range(tm // SB_TK):
            cols = slice(j * SB_TK, (j + 1) * SB_TK)
            sbk_ref[0, p, j, :, 0:SB_TK] = only_a[:, cols]
            sbk_ref[0, p, j, :, SB_TK:2 * SB_TK] = only_b[:, cols]
    half = lax.broadcasted_iota(jnp.int32, (tm, LANES), 1) < SB_HEAD_DIM
    for n, dst in enumerate((sbq_ref, sbv_ref, sbz_ref)):
        cols = slice(n * SB_WIDTH, (n + 1) * SB_WIDTH)
        res = jnp.dot(h_scr[...], wsb_ref[:, cols], preferred_element_type=F32).astype(BF16)
        for p in range(SB_PAIRS):
            blk = res[:, p * LANES:(p + 1) * LANES]
            if dst is sbv_ref:
                zero = jnp.zeros_like(blk)
                only_a = jnp.where(half, blk, zero)
                only_b = jnp.where(half, zero, blk)
                for j in range(tm // SB_TK):
                    rows = slice(j * SB_TK, (j + 1) * SB_TK)
                    dst[0, p, 2 * j * SB_TK:(2 * j + 1) * SB_TK, :] = only_a[rows]
                    dst[0, p, (2 * j + 1) * SB_TK:(2 * j + 2) * SB_TK, :] = only_b[rows]
            else:
                dst[0, p] = blk
    gates_ref[0] = jnp.dot(h_scr[...], wg_ref[...], preferred_element_type=F32)


def _inproj(x, mod3, g_pre, w_ml, w_kt, w_sb, w_g):
    bsz, s, d = x.shape
    tm = min(IN_TM, s)
    n_ml = w_ml.shape[1]
    sb_shape = jax.ShapeDtypeStruct((bsz, SB_PAIRS, s, LANES), BF16)
    sb_spec = pl.BlockSpec((1, SB_PAIRS, tm, LANES), lambda b, i: (b, 0, i, 0))
    k_shape = jax.ShapeDtypeStruct((bsz, SB_PAIRS, s // SB_TK, LANES, 2 * SB_TK), BF16)
    k_spec = pl.BlockSpec((1, SB_PAIRS, tm // SB_TK, LANES, 2 * SB_TK), lambda b, i: (b, 0, i, 0, 0))
    v_shape = jax.ShapeDtypeStruct((bsz, SB_PAIRS, 2 * s, LANES), BF16)
    v_spec = pl.BlockSpec((1, SB_PAIRS, 2 * tm, LANES), lambda b, i: (b, 0, i, 0))
    return pl.pallas_call(
        _inproj_kernel,
        grid=(bsz, s // tm),
        in_specs=[pl.BlockSpec((1, tm, d), lambda b, i: (b, i, 0)),
                  pl.BlockSpec((1, 1, d), lambda b, i: (b, 0, 0)),
                  pl.BlockSpec((1, 1, d), lambda b, i: (b, 0, 1)),
                  pl.BlockSpec((1, d), lambda b, i: (0, 0)),
                  pl.BlockSpec((d, n_ml), lambda b, i: (0, 0)),
                  pl.BlockSpec((SB_WIDTH, d), lambda b, i: (0, 0)),
                  pl.BlockSpec((d, 3 * SB_WIDTH), lambda b, i: (0, 0)),
                  pl.BlockSpec((d, LANES), lambda b, i: (0, 0))],
        out_specs=[pl.BlockSpec((1, tm, n_ml), lambda b, i: (b, i, 0)),
                   sb_spec, k_spec, v_spec, sb_spec,
                   pl.BlockSpec((1, tm, LANES), lambda b, i: (b, i, 0))],
        out_shape=[jax.ShapeDtypeStruct((bsz, s, n_ml), BF16),
                   sb_shape, k_shape, v_shape, sb_shape,
                   jax.ShapeDtypeStruct((bsz, s, LANES), F32)],
        scratch_shapes=[pltpu.VMEM((tm, d), BF16)],
        compiler_params=pltpu.CompilerParams(
            dimension_semantics=("parallel", "parallel"), vmem_limit_bytes=VMEM_LIMIT),
        name="inproj",
    )(x, mod3, mod3, g_pre.reshape(1, d), w_ml, w_kt, w_sb, w_g)


def _mlstm_kernel(u_ref, v_ref, o_ref, z_ref, g_ref, convw_ref, convb_ref, gbias_ref,
                  ghead_ref, out_ref, ubuf, q_scr, k_scr, bc_scr, qkm_scr, cl_scr, nl_scr, cp_scr,
                  hg_scr, h2_scr, c_scr, n_scr, m_scr):
    t = u_ref.shape[1]
    n_chunks = t // CHUNK
    units = [(c, h) for c in range(n_chunks) for h in range(ML_HEADS)]
    j = pl.program_id(1)

    @pl.when(j == 0)
    def _():
        ubuf[...] = jnp.zeros_like(ubuf)
        c_scr[...] = jnp.zeros_like(c_scr)
        n_scr[...] = jnp.zeros_like(n_scr)
        m_scr[...] = jnp.zeros_like(m_scr)

    u = u_ref[0].astype(F32)
    first_row = lax.broadcasted_iota(jnp.int32, (SUBLANES, u.shape[1]), 0) == 0
    part = convw_ref[0:1, :] * u
    for i in range(1, CONV_K):
        carry = ubuf[i - 1:i, :]
        ubuf[i - 1:i, :] = part[t - 1:t, :]
        delayed = pltpu.roll(part, 1, 0)
        delayed = jnp.concatenate([jnp.where(first_row, carry, delayed[:SUBLANES]),
                                   delayed[SUBLANES:]], axis=0)
        part = convw_ref[i:i + 1, :] * u + delayed
    qk = _silu(part + convb_ref[...])
    q_scr[...] = qk[:, :ML_WIDTH].astype(BF16)
    k_scr[...] = qk[:, ML_WIDTH:] * (ML_HEAD_DIM ** -0.5)

    gates = g_ref[0] + gbias_ref[...]
    pos = lax.broadcasted_iota(jnp.int32, (t, LANES), 0) & (CHUNK - 1)
    bcum = -_softplus(-gates)
    step = 1
    while step < CHUNK:
        bcum = bcum + jnp.where(pos >= step, pltpu.roll(bcum, step, 0), 0.0)
        step *= 2
    bcum = pltpu.roll(bcum, LANES - ML_HEADS, 1)
    cdiff = gates - bcum
    cm = cdiff
    step = 1
    while step < CHUNK:
        cm = jnp.maximum(cm, jnp.where(pos >= step, pltpu.roll(cm, step, 0), -jnp.inf))
        step *= 2
    c_t = cdiff.T[0:SUBLANES, :]

    m_run = m_scr[0:1, :]
    s_old, s_loc, stats = [], [], []
    for c in range(n_chunks):
        rows = slice(c * CHUNK, (c + 1) * CHUNK)
        b_c = bcum[rows]
        a_c = b_c[CHUNK - 1:CHUNK, :]
        g_c = a_c - b_c + gates[rows]
        m_loc = jnp.max(g_c, axis=0, keepdims=True)
        mm = jnp.maximum(m_run, cm[rows])
        stats.append((mm, jnp.exp(m_run - mm), jnp.exp(-(b_c + mm)), jnp.exp(g_c - m_loc)))
        m_new = jnp.maximum(a_c + m_run, m_loc)
        s_old.append(jnp.exp(a_c + m_run - m_new))
        s_loc.append(jnp.exp(m_loc - m_new))
        m_run = m_new
    m_scr[0:1, :] = m_run

    for u, (c, h) in enumerate(units):
        for i, stat in enumerate(stats[c]):
            bc_scr[u, i] = jnp.broadcast_to(stat[:, h:h + 1], (CHUNK, LANES))

    causal = (lax.broadcasted_iota(jnp.int32, (CHUNK, CHUNK), 0)
              >= lax.broadcasted_iota(jnp.int32, (CHUNK, CHUNK), 1))
    ones_rows = jnp.ones((2 * SUBLANES, CHUNK), BF16)

    for u, (c, h) in enumerate(units):
        rows = slice(c * CHUNK, (c + 1) * CHUNK)
        lanes = slice(h * ML_HEAD_DIM, (h + 1) * ML_HEAD_DIM)
        k_c = k_scr[rows, lanes]
        s_qk = lax.dot_general(q_scr[rows, lanes], k_c.astype(BF16), (((1,), (1,)), ((), ())),
                               preferred_element_type=F32)
        dw = jnp.where(causal, jnp.exp(c_t[h:h + 1, rows] - bc_scr[u, 0][:, :CHUNK]), 0.0)
        qkm_scr[u] = (s_qk * dw).astype(BF16)
        wk = (bc_scr[u, 3] * k_c).astype(BF16)
        cl_scr[u] = lax.dot_general(v_ref[0, rows, lanes], wk, (((0,), (0,)), ((), ())),
                                    preferred_element_type=F32)
        nl_scr[u] = jnp.dot(ones_rows, wk, preferred_element_type=F32)[:SUBLANES]

    for h in range(ML_HEADS):
        c_run = c_scr[h]
        n_run = n_scr[h:h + 1, :]
        for c in range(n_chunks):
            u = c * ML_HEADS + h
            cp_scr[u, 0:ML_HEAD_DIM, :] = c_run.astype(BF16)
            cp_scr[u, ML_HEAD_DIM:2 * ML_HEAD_DIM, :] = jnp.broadcast_to(
                n_run, (ML_HEAD_DIM, ML_HEAD_DIM)).astype(BF16)
            so = s_old[c][:, h:h + 1]
            sl = s_loc[c][:, h:h + 1]
            c_run = so * c_run + sl * cl_scr[u]
            n_run = so * n_run + sl * nl_scr[u, 0:1, :]
        c_scr[h] = c_run
        n_scr[h:h + 1, :] = n_run

    ones_cols = jnp.ones((CHUNK, ML_HEAD_DIM), BF16)
    for u, (c, h) in enumerate(units):
        rows = slice(c * CHUNK, (c + 1) * CHUNK)
        lanes = slice(h * ML_HEAD_DIM, (h + 1) * ML_HEAD_DIM)
        qc = lax.dot_general(q_scr[rows, lanes], cp_scr[u], (((1,), (1,)), ((), ())),
                             preferred_element_type=F32)
        kv = jnp.dot(qkm_scr[u], jnp.concatenate([v_ref[0, rows, lanes], ones_cols], axis=1),
                     preferred_element_type=F32)
        inter_w = bc_scr[u, 1]
        den = inter_w * qc[:, ML_HEAD_DIM:] + kv[:, ML_HEAD_DIM:]
        num = inter_w * qc[:, :ML_HEAD_DIM] + kv[:, :ML_HEAD_DIM]
        hh = num * (1.0 / jnp.maximum(jnp.abs(den), bc_scr[u, 2]))
        hg = _sigmoid(o_ref[0, rows, lanes].astype(F32)) * hh
        hg_scr[u] = hg
        h2_scr[u] = (hg * hg).astype(BF16)

    ones_sq = jnp.ones((ML_HEAD_DIM, ML_HEAD_DIM), BF16)
    for u, (c, h) in enumerate(units):
        rows = slice(c * CHUNK, (c + 1) * CHUNK)
        lanes = slice(h * ML_HEAD_DIM, (h + 1) * ML_HEAD_DIM)
        ss = jnp.dot(h2_scr[u], ones_sq, preferred_element_type=F32)
        hn = hg_scr[u] * lax.rsqrt(ss * (1.0 / ML_HEAD_DIM) + EPS) * ghead_ref[:, lanes]
        out_ref[0, rows, lanes] = (hn * _silu(z_ref[0, rows, lanes].astype(F32))).astype(BF16)


def _mlstm(ml, gates, conv_w, conv_b, gbias, g_head):
    bsz, s, _ = ml.shape
    t = min(ML_T, s)
    w2 = 2 * ML_WIDTH
    n_units = (t // CHUNK) * ML_HEADS
    return pl.pallas_call(
        _mlstm_kernel,
        grid=(bsz, s // t),
        in_specs=[pl.BlockSpec((1, t, w2), lambda b, j: (b, j, 0)),
                  pl.BlockSpec((1, t, ML_WIDTH), lambda b, j: (b, j, 2)),
                  pl.BlockSpec((1, t, ML_WIDTH), lambda b, j: (b, j, 3)),
                  pl.BlockSpec((1, t, ML_WIDTH), lambda b, j: (b, j, 4)),
                  pl.BlockSpec((1, t, LANES), lambda b, j: (b, j, 0)),
                  pl.BlockSpec((CONV_K, w2), lambda b, j: (0, 0)),
                  pl.BlockSpec((1, w2), lambda b, j: (0, 0)),
                  pl.BlockSpec((1, LANES), lambda b, j: (0, 0)),
                  pl.BlockSpec((1, ML_WIDTH), lambda b, j: (0, 0))],
        out_specs=pl.BlockSpec((1, t, ML_WIDTH), lambda b, j: (b, j, 0)),
        out_shape=jax.ShapeDtypeStruct((bsz, s, ML_WIDTH), BF16),
        scratch_shapes=[pltpu.VMEM((SUBLANES, w2), F32),
                        pltpu.VMEM((t, ML_WIDTH), BF16),
                        pltpu.VMEM((t, ML_WIDTH), F32),
                        pltpu.VMEM((n_units, 4, CHUNK, LANES), F32),
                        pltpu.VMEM((n_units, CHUNK, CHUNK), BF16),
                        pltpu.VMEM((n_units, ML_HEAD_DIM, ML_HEAD_DIM), F32),
                        pltpu.VMEM((n_units, SUBLANES, ML_HEAD_DIM), F32),
                        pltpu.VMEM((n_units, 2 * ML_HEAD_DIM, ML_HEAD_DIM), BF16),
                        pltpu.VMEM((n_units, CHUNK, ML_HEAD_DIM), F32),
                        pltpu.VMEM((n_units, CHUNK, ML_HEAD_DIM), BF16),
                        pltpu.VMEM((ML_HEADS, ML_HEAD_DIM, ML_HEAD_DIM), F32),
                        pltpu.VMEM((SUBLANES, ML_HEAD_DIM), F32),
                        pltpu.VMEM((SUBLANES, LANES), F32)],
        compiler_params=pltpu.CompilerParams(
            dimension_semantics=("parallel", "arbitrary"), vmem_limit_bytes=VMEM_LIMIT),
        name="mlstm",
    )(ml, ml, ml, ml, gates, conv_w, conv_b.reshape(1, w2), gbias, g_head.reshape(1, ML_WIDTH))


def _sb_sweeps(ds, first, qt0, chains, q_ref, k_ref, v_ref, cum_ref, z_scr, n_scr, a_scr, s_scr,
               acc_scr, r_scr):
    n_chain = len(chains)
    items = [(i * n_chain + c, d, first and i == 0, g, p)
             for i, d in enumerate(ds) for c, (g, p) in enumerate(chains)]
    if first:
        strict = ((lax.broadcasted_iota(jnp.int32, (SB_TQ, 2 * SB_TK), 1) & (SB_TK - 1))
                  < lax.broadcasted_iota(jnp.int32, (SB_TQ, 2 * SB_TK), 0))
    lane_a = lax.broadcasted_iota(jnp.int32, (SB_TQ, LANES), 1) < SB_HEAD_DIM

    def key_rows(g, d):
        return pl.ds(jnp.maximum(qt0 + g - d, 0) * (2 * SB_TK), 2 * SB_TK)

    def item_rows(slot):
        return slice(slot * SB_TQ, (slot + 1) * SB_TQ)

    for slot, d, diag, g, p in items:
        q2 = q_ref[0, p, g * SB_TQ:(g + 1) * SB_TQ, :]
        z = jnp.dot(q2, k_ref[0, p, jnp.maximum(qt0 + g - d, 0)],
                    preferred_element_type=F32)
        nlk = jnp.log(1.0 + jnp.exp2(jnp.minimum(z, SB_ZMAX))) * LOG2E
        if diag:
            nlk = jnp.where(strict, nlk, 0.0)
        z_scr[slot] = z
        n_scr[item_rows(slot), :] = nlk.astype(BF16)

    m_all = jnp.dot(n_scr[0:len(items) * SB_TQ, :], cum_ref[...], preferred_element_type=F32)
    for slot, d, diag, g, p in items:
        m = m_all[item_rows(slot)]
        a = jnp.exp2(z_scr[slot] - m)
        if diag:
            a = jnp.where(strict, a, 0.0)
        a_scr[slot] = a.astype(BF16)
        s_scr[slot] = jnp.where(lane_a, m[:, 0:1], m[:, SB_TK:SB_TK + 1])

    rmin = None
    for c, (g, p) in enumerate(chains):
        acc, r = (None, None) if first else (acc_scr[c], r_scr[c])
        for i, d in enumerate(ds):
            slot = i * n_chain + c
            pv = jnp.dot(a_scr[slot], v_ref[0, p, key_rows(g, d), :], preferred_element_type=F32)
            if first and i == 0:
                acc, r = pv, s_scr[slot]
            else:
                r = jnp.where(qt0 + g - d >= 0, r, SB_DEAD)
                acc = acc + jnp.exp2(-r) * pv
                r = r + s_scr[slot]
        acc_scr[c] = acc
        r_scr[c] = r
        rmin = r if rmin is None else jnp.minimum(rmin, r)
    return jnp.min(rmin)


def _sb_kernel(q_ref, k_ref, v_ref, z_ref, cum_ref, o_ref, z_scr, n_scr, a_scr, s_scr, acc_scr, r_scr):
    n_sub = q_ref.shape[2] // SB_TQ
    qt0 = pl.program_id(1) * n_sub
    chains = [(g, p) for g in range(n_sub) for p in range(SB_PAIRS)]
    sweeps = functools.partial(_sb_sweeps, qt0=qt0, chains=chains, q_ref=q_ref, k_ref=k_ref,
                               v_ref=v_ref, cum_ref=cum_ref, z_scr=z_scr, n_scr=n_scr, a_scr=a_scr,
                               s_scr=s_scr, acc_scr=acc_scr, r_scr=r_scr)

    def cond(carry):
        d, rm = carry
        return jnp.logical_and(qt0 + (n_sub - 1) - d >= 0, rm < SB_SKIP)

    def body(carry):
        d, _ = carry
        return d + 1, sweeps([d], False)

    rm = sweeps(list(range(SB_MIN_SWEEPS)), True)
    lax.while_loop(cond, body, (jnp.int32(SB_MIN_SWEEPS), rm))

    for c, (g, p) in enumerate(chains):
        zc = z_ref[0, p, g * SB_TQ:(g + 1) * SB_TQ, :].astype(F32)
        o_ref[0, p, g * SB_TQ:(g + 1) * SB_TQ, :] = (acc_scr[c] * _silu(zc)).astype(BF16)


def _stick_breaking(sbq, sbk, sbv, sbz):
    bsz, pairs, s, _ = sbq.shape
    rows = min(SB_ROWS, s)
    j = lax.broadcasted_iota(jnp.int32, (2 * SB_TK, 2 * SB_TK), 0)
    c = lax.broadcasted_iota(jnp.int32, (2 * SB_TK, 2 * SB_TK), 1)
    cum = jnp.where((j >= c) & ((j // SB_TK) == (c // SB_TK)), 1.0, 0.0).astype(BF16)
    q_spec = pl.BlockSpec((1, pairs, rows, LANES), lambda b, i: (b, 0, i, 0))
    k_spec = pl.BlockSpec((1, pairs, s // SB_TK, LANES, 2 * SB_TK), lambda b, i: (b, 0, 0, 0, 0))
    v_spec = pl.BlockSpec((1, pairs, 2 * s, LANES), lambda b, i: (b, 0, 0, 0))
    n_chain = (rows // SB_TQ) * pairs
    n_item = SB_MIN_SWEEPS * n_chain
    return pl.pallas_call(
        _sb_kernel,
        grid=(bsz, s // rows),
        in_specs=[q_spec, k_spec, v_spec, q_spec,
                  pl.BlockSpec((2 * SB_TK, 2 * SB_TK), lambda b, i: (0, 0))],
        out_specs=q_spec,
        out_shape=jax.ShapeDtypeStruct((bsz, pairs, s, LANES), BF16),
        scratch_shapes=[pltpu.VMEM((n_item, SB_TQ, 2 * SB_TK), F32),
                        pltpu.VMEM((n_item * SB_TQ, 2 * SB_TK), BF16),
                        pltpu.VMEM((n_item, SB_TQ, 2 * SB_TK), BF16),
                        pltpu.VMEM((n_item, SB_TQ, LANES), F32),
                        pltpu.VMEM((n_chain, SB_TQ, LANES), F32),
                        pltpu.VMEM((n_chain, SB_TQ, LANES), F32)],
        compiler_params=pltpu.CompilerParams(
            dimension_semantics=("parallel", "arbitrary"), vmem_limit_bytes=VMEM_LIMIT),
        name="stickbreak",
    )(sbq, sbk, sbv, sbz, cum)


def _outproj_kernel(hml_ref, hsb_ref, x_ref, gate_ref, gpost_ref, wml_ref, wsb_ref, o_ref):
    hsb = jnp.concatenate([hsb_ref[0, p] for p in range(SB_PAIRS)], axis=1)
    y = (jnp.dot(hml_ref[0], wml_ref[...], preferred_element_type=F32)
         + jnp.dot(hsb, wsb_ref[...], preferred_element_type=F32))
    r = lax.rsqrt(jnp.mean(y * y, axis=-1, keepdims=True) + EPS)
    o_ref[0] = x_ref[0] + gate_ref[0] * ((y * r) * gpost_ref[...])


def _outproj(hml, hsb, x, mod3, g_post, w_oml, w_osb):
    bsz, s, d = x.shape
    tm = min(OUT_TM, s)
    return pl.pallas_call(
        _outproj_kernel,
        grid=(bsz, s // tm),
        in_specs=[pl.BlockSpec((1, tm, ML_WIDTH), lambda b, i: (b, i, 0)),
                  pl.BlockSpec((1, SB_PAIRS, tm, LANES), lambda b, i: (b, 0, i, 0)),
                  pl.BlockSpec((1, tm, d), lambda b, i: (b, i, 0)),
                  pl.BlockSpec((1, 1, d), lambda b, i: (b, 0, 2)),
                  pl.BlockSpec((1, d), lambda b, i: (0, 0)),
                  pl.BlockSpec((ML_WIDTH, d), lambda b, i: (0, 0)),
                  pl.BlockSpec((SB_WIDTH, d), lambda b, i: (0, 0))],
        out_specs=pl.BlockSpec((1, tm, d), lambda b, i: (b, i, 0)),
        out_shape=jax.ShapeDtypeStruct((bsz, s, d), F32),
        compiler_params=pltpu.CompilerParams(
            dimension_semantics=("parallel", "parallel"), vmem_limit_bytes=VMEM_LIMIT),
        name="outproj",
    )(hml, hsb, x, mod3, g_post.reshape(1, d), w_oml, w_osb)


def kernel(x, c, w_ada, b_ada, g_pre, w_in, b_igate, b_fgate, conv_w, conv_b, g_ml_head, w_out, g_post):
    bsz, s, d = x.shape
    o_gate = 2 * ML_WIDTH + 3 * ML_WIDTH
    o_sb = o_gate + 2 * ML_HEADS
    w_ml = w_in[:, :o_gate].astype(BF16)
    w_g = jnp.pad(w_in[:, o_gate:o_sb], ((0, 0), (0, LANES - 2 * ML_HEADS))).astype(BF16)
    sb_q, sb_k, sb_v, sb_z = (w_in[:, o_sb + n * SB_WIDTH:o_sb + (n + 1) * SB_WIDTH] for n in range(4))
    w_sb = jnp.concatenate([sb_q * (LOG2E * SB_HEAD_DIM ** -0.5), sb_v, sb_z], axis=1).astype(BF16)
    w_kt = sb_k.T.astype(BF16)
    gbias = jnp.pad(jnp.concatenate([b_igate, b_fgate]), (0, LANES - 2 * ML_HEADS)).reshape(1, LANES)
    w_o = w_out.astype(BF16)

    mod3 = _adaln(c, w_ada, b_ada).reshape(bsz, 1, 3 * d)
    ml, sbq, sbk, sbv, sbz, gates = _inproj(x, mod3, g_pre, w_ml, w_kt, w_sb, w_g)
    hml = _mlstm(ml, gates, conv_w, conv_b, gbias, g_ml_head)
    hsb = _stick_breaking(sbq, sbk, sbv, sbz)
    return _outproj(hml, hsb, x, mod3, g_post, w_o[:ML_WIDTH], w_o[ML_WIDTH:])
```

```python
import functools

import jax
import jax.numpy as jnp
from jax import lax
from jax.experimental import pallas as pl
from jax.experimental.pallas import tpu as pltpu

F32 = jnp.float32
BF16 = jnp.bfloat16
HIGHEST = lax.Precision.HIGHEST

EPS = 1e-6
ML_HEADS = 4
ML_HEAD_DIM = 128
ML_WIDTH = ML_HEADS * ML_HEAD_DIM
SB_HEADS = 8
SB_HEAD_DIM = 64
SB_WIDTH = SB_HEADS * SB_HEAD_DIM
SB_PAIRS = SB_HEADS // 2
CONV_K = 4
CHUNK = 64
LANES = 128
SUBLANES = 8

IN_TM = 512
ML_T = 512
SB_TQ = 128
SB_TK = 128
SB_ROWS = 256
OUT_TM = 2048
LOG2E = 1.4426950408889634
SB_SKIP = 128.0
SB_DEAD = 1e30
SB_MIN_SWEEPS = 3
SB_ZMAX = 126.0
VMEM_LIMIT = 56 * 1024 * 1024


def _sigmoid(x):
    return 0.5 + 0.5 * jnp.tanh(0.5 * x)


def _silu(x):
    hx = 0.5 * x
    return hx + hx * jnp.tanh(hx)


def _softplus(x):
    return jnp.maximum(x, 0.0) + jnp.log(1.0 + jnp.exp(-jnp.abs(x)))


def _adaln_kernel(c_ref, w_ref, b_ref, o_ref):
    c = c_ref[...]
    s = _silu(c)
    o_ref[...] = jnp.dot(s, w_ref[...], preferred_element_type=F32, precision=HIGHEST) + b_ref[...]


def _adaln(c, w_ada, b_ada):
    bsz, d = c.shape
    n = w_ada.shape[1]
    return pl.pallas_call(
        _adaln_kernel,
        grid=(n // d,),
        in_specs=[pl.BlockSpec((bsz, d), lambda j: (0, 0)),
                  pl.BlockSpec((d, d), lambda j: (0, j)),
                  pl.BlockSpec((1, d), lambda j: (0, j))],
        out_specs=pl.BlockSpec((bsz, d), lambda j: (0, j)),
        out_shape=jax.ShapeDtypeStruct((bsz, n), F32),
        name="adaln",
    )(c, w_ada, b_ada.reshape(1, n))


def _inproj_kernel(x_ref, shift_ref, scale_ref, gpre_ref, wml_ref, wkt_ref, wsb_ref,
                   ml_ref, sbq_ref, sbk_ref, sbv_ref, sbz_ref, gates_ref, h_scr):
    x = x_ref[0]
    r = lax.rsqrt(jnp.mean(x * x, axis=-1, keepdims=True) + EPS)
    a = gpre_ref[...] * (1.0 + scale_ref[0])
    h_scr[...] = ((x * r) * a + shift_ref[0]).astype(BF16)

    n_ml = wml_ref.shape[1]
    for n in range(n_ml // 512):
        cols = slice(n * 512, (n + 1) * 512)
        ml_ref[0, :, cols] = jnp.dot(h_scr[...], wml_ref[:, cols],
                                     preferred_element_type=F32).astype(BF16)
    tm = x.shape[0]
    kt = lax.dot_general(wkt_ref[...], h_scr[...], (((1,), (1,)), ((), ())),
                         preferred_element_type=F32)
    gates_ref[0] = kt[SB_WIDTH:SB_WIDTH + SUBLANES, :]
    kt = kt[:SB_WIDTH].astype(BF16)
    d_a = lax.broadcasted_iota(jnp.int32, (LANES, tm), 0) < SB_HEAD_DIM
    for p in range(SB_PAIRS):
        blk = kt[p * LANES:(p + 1) * LANES, :]
        zero = jnp.zeros_like(blk)
        only_a = jnp.where(d_a, blk, zero)
        only_b = jnp.where(d_a, zero, blk)
        for j in range(tm // SB_TK):
            cols = slice(j * SB_TK, (j + 1) * SB_TK)
            sbk_ref[0, p, j, :, 0:SB_TK] = only_a[:, cols]
            sbk_ref[0, p, j, :, SB_TK:2 * SB_TK] = only_b[:, cols]
    half = lax.broadcasted_iota(jnp.int32, (tm, LANES), 1) < SB_HEAD_DIM
    for n, dst in enumerate((sbq_ref, sbv_ref, sbz_ref)):
        cols = slice(n * SB_WIDTH, (n + 1) * SB_WIDTH)
        res = jnp.dot(h_scr[...], wsb_ref[:, cols], preferred_element_type=F32).astype(BF16)
        for p in range(SB_PAIRS):
            blk = res[:, p * LANES:(p + 1) * LANES]
            if dst is sbv_ref:
                zero = jnp.zeros_like(blk)
                only_a = jnp.where(half, blk, zero)
                only_b = jnp.where(half, zero, blk)
                for j in range(tm // SB_TK):
                    rows = slice(j * SB_TK, (j + 1) * SB_TK)
                    dst[0, p, 2 * j * SB_TK:(2 * j + 1) * SB_TK, :] = only_a[rows]
                    dst[0, p, (2 * j + 1) * SB_TK:(2 * j + 2) * SB_TK, :] = only_b[rows]
            else:
                dst[0, p] = blk


def _inproj(x, mod3, g_pre, w_ml, w_kt, w_sb):
    bsz, s, d = x.shape
    tm = min(IN_TM, s)
    n_ml = w_ml.shape[1]
    sb_shape = jax.ShapeDtypeStruct((bsz, SB_PAIRS, s, LANES), BF16)
    sb_spec = pl.BlockSpec((1, SB_PAIRS, tm, LANES), lambda b, i: (b, 0, i, 0))
    k_shape = jax.ShapeDtypeStruct((bsz, SB_PAIRS, s // SB_TK, LANES, 2 * SB_TK), BF16)
    k_spec = pl.BlockSpec((1, SB_PAIRS, tm // SB_TK, LANES, 2 * SB_TK), lambda b, i: (b, 0, i, 0, 0))
    v_shape = jax.ShapeDtypeStruct((bsz, SB_PAIRS, 2 * s, LANES), BF16)
    v_spec = pl.BlockSpec((1, SB_PAIRS, 2 * tm, LANES), lambda b, i: (b, 0, i, 0))
    return pl.pallas_call(
        _inproj_kernel,
        grid=(bsz, s // tm),
        in_specs=[pl.BlockSpec((1, tm, d), lambda b, i: (b, i, 0)),
                  pl.BlockSpec((1, 1, d), lambda b, i: (b, 0, 0)),
                  pl.BlockSpec((1, 1, d), lambda b, i: (b, 0, 1)),
                  pl.BlockSpec((1, d), lambda b, i: (0, 0)),
                  pl.BlockSpec((d, n_ml), lambda b, i: (0, 0)),
                  pl.BlockSpec(w_kt.shape, lambda b, i: (0, 0)),
                  pl.BlockSpec((d, 3 * SB_WIDTH), lambda b, i: (0, 0))],
        out_specs=[pl.BlockSpec((1, tm, n_ml), lambda b, i: (b, i, 0)),
                   sb_spec, k_spec, v_spec, sb_spec,
                   pl.BlockSpec((1, SUBLANES, tm), lambda b, i: (b, 0, i))],
        out_shape=[jax.ShapeDtypeStruct((bsz, s, n_ml), BF16),
                   sb_shape, k_shape, v_shape, sb_shape,
                   jax.ShapeDtypeStruct((bsz, SUBLANES, s), F32)],
        scratch_shapes=[pltpu.VMEM((tm, d), BF16)],
        compiler_params=pltpu.CompilerParams(
            dimension_semantics=("parallel", "parallel"), vmem_limit_bytes=VMEM_LIMIT),
        name="inproj",
    )(x, mod3, mod3, g_pre.reshape(1, d), w_ml, w_kt, w_sb)


def _mlstm_kernel(u_ref, v_ref, o_ref, z_ref, g_ref, convw_ref, convb_ref, gbias_ref,
                  ghead_ref, out_ref, ubuf, q_scr, k_scr, bc_scr, qkm_scr, cl_scr, nl_scr, cp_scr,
                  hg_scr, h2_scr, c_scr, n_scr, m_scr):
    t = u_ref.shape[1]
    n_chunks = t // CHUNK
    units = [(c, h) for c in range(n_chunks) for h in range(ML_HEADS)]
    j = pl.program_id(1)

    @pl.when(j == 0)
    def _():
        ubuf[...] = jnp.zeros_like(ubuf)
        c_scr[...] = jnp.zeros_like(c_scr)
        n_scr[...] = jnp.zeros_like(n_scr)
        m_scr[...] = jnp.zeros_like(m_scr)

    u = u_ref[0].astype(F32)
    first_row = lax.broadcasted_iota(jnp.int32, (SUBLANES, u.shape[1]), 0) == 0
    part = convw_ref[0:1, :] * u
    for i in range(1, CONV_K):
        carry = ubuf[i - 1:i, :]
        ubuf[i - 1:i, :] = part[t - 1:t, :]
        delayed = pltpu.roll(part, 1, 0)
        delayed = jnp.concatenate([jnp.where(first_row, carry, delayed[:SUBLANES]),
                                   delayed[SUBLANES:]], axis=0)
        part = convw_ref[i:i + 1, :] * u + delayed
    qk = _silu(part + convb_ref[...])
    q_scr[...] = qk[:, :ML_WIDTH].astype(BF16)
    k_scr[...] = qk[:, ML_WIDTH:] * (ML_HEAD_DIM ** -0.5)

    gates = jnp.concatenate([g_ref[0], jnp.zeros((LANES - SUBLANES, t), F32)], axis=0).T
    gates = gates + gbias_ref[...]
    pos = lax.broadcasted_iota(jnp.int32, (t, LANES), 0) & (CHUNK - 1)
    bcum = -_softplus(-gates)
    step = 1
    while step < CHUNK:
        bcum = bcum + jnp.where(pos >= step, pltpu.roll(bcum, step, 0), 0.0)
        step *= 2
    bcum = pltpu.roll(bcum, LANES - ML_HEADS, 1)
    cdiff = gates - bcum
    cm = cdiff
    step = 1
    while step < CHUNK:
        cm = jnp.maximum(cm, jnp.where(pos >= step, pltpu.roll(cm, step, 0), -jnp.inf))
        step *= 2
    c_t = cdiff.T[0:SUBLANES, :]

    m_run = m_scr[0:1, :]
    s_old, s_loc, stats = [], [], []
    for c in range(n_chunks):
        rows = slice(c * CHUNK, (c + 1) * CHUNK)
        b_c = bcum[rows]
        a_c = b_c[CHUNK - 1:CHUNK, :]
        g_c = a_c - b_c + gates[rows]
        m_loc = jnp.max(g_c, axis=0, keepdims=True)
        mm = jnp.maximum(m_run, cm[rows])
        stats.append((mm, jnp.exp(m_run - mm), jnp.exp(-(b_c + mm)), jnp.exp(g_c - m_loc)))
        m_new = jnp.maximum(a_c + m_run, m_loc)
        s_old.append(jnp.exp(a_c + m_run - m_new))
        s_loc.append(jnp.exp(m_loc - m_new))
        m_run = m_new
    m_scr[0:1, :] = m_run

    for u, (c, h) in enumerate(units):
        for i, stat in enumerate(stats[c]):
            bc_scr[u, i] = jnp.broadcast_to(stat[:, h:h + 1], (CHUNK, LANES))

    causal = (lax.broadcasted_iota(jnp.int32, (CHUNK, CHUNK), 0)
              >= lax.broadcasted_iota(jnp.int32, (CHUNK, CHUNK), 1))
    ones_rows = jnp.ones((2 * SUBLANES, CHUNK), BF16)

    for u, (c, h) in enumerate(units):
        rows = slice(c * CHUNK, (c + 1) * CHUNK)
        lanes = slice(h * ML_HEAD_DIM, (h + 1) * ML_HEAD_DIM)
        k_c = k_scr[rows, lanes]
        s_qk = lax.dot_general(q_scr[rows, lanes], k_c.astype(BF16), (((1,), (1,)), ((), ())),
                               preferred_element_type=F32)
        dw = jnp.where(causal, jnp.exp(c_t[h:h + 1, rows] - bc_scr[u, 0][:, :CHUNK]), 0.0)
        qkm_scr[u] = (s_qk * dw).astype(BF16)
        wk = (bc_scr[u, 3] * k_c).astype(BF16)
        cl_scr[u] = lax.dot_general(v_ref[0, rows, lanes], wk, (((0,), (0,)), ((), ())),
                                    preferred_element_type=F32)
        nl_scr[u] = jnp.dot(ones_rows, wk, preferred_element_type=F32)[:SUBLANES]

    for h in range(ML_HEADS):
        c_run = c_scr[h]
        n_run = n_scr[h:h + 1, :]
        for c in range(n_chunks):
            u = c * ML_HEADS + h
            cp_scr[u, 0:ML_HEAD_DIM, :] = c_run.astype(BF16)
            cp_scr[u, ML_HEAD_DIM:2 * ML_HEAD_DIM, :] = jnp.broadcast_to(
                n_run, (ML_HEAD_DIM, ML_HEAD_DIM)).astype(BF16)
            so = s_old[c][:, h:h + 1]
            sl = s_loc[c][:, h:h + 1]
            c_run = so * c_run + sl * cl_scr[u]
            n_run = so * n_run + sl * nl_scr[u, 0:1, :]
        c_scr[h] = c_run
        n_scr[h:h + 1, :] = n_run

    ones_cols = jnp.ones((CHUNK, ML_HEAD_DIM), BF16)
    for u, (c, h) in enumerate(units):
        rows = slice(c * CHUNK, (c + 1) * CHUNK)
        lanes = slice(h * ML_HEAD_DIM, (h + 1) * ML_HEAD_DIM)
        qc = lax.dot_general(q_scr[rows, lanes], cp_scr[u], (((1,), (1,)), ((), ())),
                             preferred_element_type=F32)
        kv = jnp.dot(qkm_scr[u], jnp.concatenate([v_ref[0, rows, lanes], ones_cols], axis=1),
                     preferred_element_type=F32)
        inter_w = bc_scr[u, 1]
        den = inter_w * qc[:, ML_HEAD_DIM:] + kv[:, ML_HEAD_DIM:]
        num = inter_w * qc[:, :ML_HEAD_DIM] + kv[:, :ML_HEAD_DIM]
        hh = num * (1.0 / jnp.maximum(jnp.abs(den), bc_scr[u, 2]))
        hg = _sigmoid(o_ref[0, rows, lanes].astype(F32)) * hh
        hg_scr[u] = hg
        h2_scr[u] = (hg * hg).astype(BF16)

    ones_sq = jnp.ones((ML_HEAD_DIM, ML_HEAD_DIM), BF16)
    for u, (c, h) in enumerate(units):
        rows = slice(c * CHUNK, (c + 1) * CHUNK)
        lanes = slice(h * ML_HEAD_DIM, (h + 1) * ML_HEAD_DIM)
        ss = jnp.dot(h2_scr[u], ones_sq, preferred_element_type=F32)
        hn = hg_scr[u] * lax.rsqrt(ss * (1.0 / ML_HEAD_DIM) + EPS) * ghead_ref[:, lanes]
        out_ref[0, rows, lanes] = (hn * _silu(z_ref[0, rows, lanes].astype(F32))).astype(BF16)


def _mlstm(ml, gates, conv_w, conv_b, gbias, g_head):
    bsz, s, _ = ml.shape
    t = min(ML_T, s)
    w2 = 2 * ML_WIDTH
    n_units = (t // CHUNK) * ML_HEADS
    return pl.pallas_call(
        _mlstm_kernel,
        grid=(bsz, s // t),
        in_specs=[pl.BlockSpec((1, t, w2), lambda b, j: (b, j, 0)),
                  pl.BlockSpec((1, t, ML_WIDTH), lambda b, j: (b, j, 2)),
                  pl.BlockSpec((1, t, ML_WIDTH), lambda b, j: (b, j, 3)),
                  pl.BlockSpec((1, t, ML_WIDTH), lambda b, j: (b, j, 4)),
                  pl.BlockSpec((1, SUBLANES, t), lambda b, j: (b, 0, j)),
                  pl.BlockSpec((CONV_K, w2), lambda b, j: (0, 0)),
                  pl.BlockSpec((1, w2), lambda b, j: (0, 0)),
                  pl.BlockSpec((1, LANES), lambda b, j: (0, 0)),
                  pl.BlockSpec((1, ML_WIDTH), lambda b, j: (0, 0))],
        out_specs=pl.BlockSpec((1, t, ML_WIDTH), lambda b, j: (b, j, 0)),
        out_shape=jax.ShapeDtypeStruct((bsz, s, ML_WIDTH), BF16),
        scratch_shapes=[pltpu.VMEM((SUBLANES, w2), F32),
                        pltpu.VMEM((t, ML_WIDTH), BF16),
                        pltpu.VMEM((t, ML_WIDTH), F32),
                        pltpu.VMEM((n_units, 4, CHUNK, LANES), F32),
                        pltpu.VMEM((n_units, CHUNK, CHUNK), BF16),
                        pltpu.VMEM((n_units, ML_HEAD_DIM, ML_HEAD_DIM), F32),
                        pltpu.VMEM((n_units, SUBLANES, ML_HEAD_DIM), F32),
                        pltpu.VMEM((n_units, 2 * ML_HEAD_DIM, ML_HEAD_DIM), BF16),
                        pltpu.VMEM((n_units, CHUNK, ML_HEAD_DIM), F32),
                        pltpu.VMEM((n_units, CHUNK, ML_HEAD_DIM), BF16),
                        pltpu.VMEM((ML_HEADS, ML_HEAD_DIM, ML_HEAD_DIM), F32),
                        pltpu.VMEM((SUBLANES, ML_HEAD_DIM), F32),
                        pltpu.VMEM((SUBLANES, LANES), F32)],
        compiler_params=pltpu.CompilerParams(
            dimension_semantics=("parallel", "arbitrary"), vmem_limit_bytes=VMEM_LIMIT),
        name="mlstm",
    )(ml, ml, ml, ml, gates, conv_w, conv_b.reshape(1, w2), gbias, g_head.reshape(1, ML_WIDTH))


def _sb_sweeps(ds, first, qt0, chains, q_ref, k_ref, v_ref, cum_ref, z_scr, n_scr, a_scr, s_scr,
               acc_scr, r_scr):
    n_chain = len(chains)
    items = [(i * n_chain + c, d, first and i == 0, g, p)
             for i, d in enumerate(ds) for c, (g, p) in enumerate(chains)]
    if first:
        strict = ((lax.broadcasted_iota(jnp.int32, (SB_TQ, 2 * SB_TK), 1) & (SB_TK - 1))
                  < lax.broadcasted_iota(jnp.int32, (SB_TQ, 2 * SB_TK), 0))
    lane_a = lax.broadcasted_iota(jnp.int32, (SB_TQ, LANES), 1) < SB_HEAD_DIM

    def key_rows(g, d):
        return pl.ds(jnp.maximum(qt0 + g - d, 0) * (2 * SB_TK), 2 * SB_TK)

    def item_rows(slot):
        return slice(slot * SB_TQ, (slot + 1) * SB_TQ)

    for slot, d, diag, g, p in items:
        q2 = q_ref[0, p, g * SB_TQ:(g + 1) * SB_TQ, :]
        z = jnp.dot(q2, k_ref[0, p, jnp.maximum(qt0 + g - d, 0)],
                    preferred_element_type=F32)
        nlk = jnp.log(1.0 + jnp.exp2(jnp.minimum(z, SB_ZMAX))) * LOG2E
        if diag:
            nlk = jnp.where(strict, nlk, 0.0)
        z_scr[slot] = z
        n_scr[item_rows(slot), :] = nlk.astype(BF16)

    m_all = jnp.dot(n_scr[0:len(items) * SB_TQ, :], cum_ref[...], preferred_element_type=F32)
    for slot, d, diag, g, p in items:
        m = m_all[item_rows(slot)]
        a = jnp.exp2(z_scr[slot] - m)
        if diag:
            a = jnp.where(strict, a, 0.0)
        a_scr[slot] = a.astype(BF16)
        s_scr[slot] = jnp.where(lane_a, m[:, 0:1], m[:, SB_TK:SB_TK + 1])

    rmin = None
    for c, (g, p) in enumerate(chains):
        acc, r = (None, None) if first else (acc_scr[c], r_scr[c])
        for i, d in enumerate(ds):
            slot = i * n_chain + c
            pv = jnp.dot(a_scr[slot], v_ref[0, p, key_rows(g, d), :], preferred_element_type=F32)
            if first and i == 0:
                acc, r = pv, s_scr[slot]
            else:
                r = jnp.where(qt0 + g - d >= 0, r, SB_DEAD)
                acc = acc + jnp.exp2(-r) * pv
                r = r + s_scr[slot]
        acc_scr[c] = acc
        r_scr[c] = r
        rmin = r if rmin is None else jnp.minimum(rmin, r)
    return jnp.min(rmin)


def _sb_kernel(q_ref, k_ref, v_ref, z_ref, cum_ref, o_ref, z_scr, n_scr, a_scr, s_scr, acc_scr, r_scr):
    n_sub = q_ref.shape[2] // SB_TQ
    qt0 = pl.program_id(1) * n_sub
    chains = [(g, p) for g in range(n_sub) for p in range(SB_PAIRS)]
    sweeps = functools.partial(_sb_sweeps, qt0=qt0, chains=chains, q_ref=q_ref, k_ref=k_ref,
                               v_ref=v_ref, cum_ref=cum_ref, z_scr=z_scr, n_scr=n_scr, a_scr=a_scr,
                               s_scr=s_scr, acc_scr=acc_scr, r_scr=r_scr)

    def cond(carry):
        d, rm = carry
        return jnp.logical_and(qt0 + (n_sub - 1) - d >= 0, rm < SB_SKIP)

    def body(carry):
        d, _ = carry
        return d + 1, sweeps([d], False)

    rm = sweeps(list(range(SB_MIN_SWEEPS)), True)
    lax.while_loop(cond, body, (jnp.int32(SB_MIN_SWEEPS), rm))

    for c, (g, p) in enumerate(chains):
        zc = z_ref[0, p, g * SB_TQ:(g + 1) * SB_TQ, :].astype(F32)
        o_ref[0, p, g * SB_TQ:(g + 1) * SB_TQ, :] = (acc_scr[c] * _silu(zc)).astype(BF16)


def _stick_breaking(sbq, sbk, sbv, sbz):
    bsz, pairs, s, _ = sbq.shape
    rows = min(SB_ROWS, s)
    j = lax.broadcasted_iota(jnp.int32, (2 * SB_TK, 2 * SB_TK), 0)
    c = lax.broadcasted_iota(jnp.int32, (2 * SB_TK, 2 * SB_TK), 1)
    cum = jnp.where((j >= c) & ((j // SB_TK) == (c // SB_TK)), 1.0, 0.0).astype(BF16)
    q_spec = pl.BlockSpec((1, pairs, rows, LANES), lambda b, i: (b, 0, i, 0))
    k_spec = pl.BlockSpec((1, pairs, s // SB_TK, LANES, 2 * SB_TK), lambda b, i: (b, 0, 0, 0, 0))
    v_spec = pl.BlockSpec((1, pairs, 2 * s, LANES), lambda b, i: (b, 0, 0, 0))
    n_chain = (rows // SB_TQ) * pairs
    n_item = SB_MIN_SWEEPS * n_chain
    return pl.pallas_call(
        _sb_kernel,
        grid=(bsz, s // rows),
        in_specs=[q_spec, k_spec, v_spec, q_spec,
                  pl.BlockSpec((2 * SB_TK, 2 * SB_TK), lambda b, i: (0, 0))],
        out_specs=q_spec,
        out_shape=jax.ShapeDtypeStruct((bsz, pairs, s, LANES), BF16),
        scratch_shapes=[pltpu.VMEM((n_item, SB_TQ, 2 * SB_TK), F32),
                        pltpu.VMEM((n_item * SB_TQ, 2 * SB_TK), BF16),
                        pltpu.VMEM((n_item, SB_TQ, 2 * SB_TK), BF16),
                        pltpu.VMEM((n_item, SB_TQ, LANES), F32),
                        pltpu.VMEM((n_chain, SB_TQ, LANES), F32),
                        pltpu.VMEM((n_chain, SB_TQ, LANES), F32)],
        compiler_params=pltpu.CompilerParams(
            dimension_semantics=("parallel", "arbitrary"), vmem_limit_bytes=VMEM_LIMIT),
        name="stickbreak",
    )(sbq, sbk, sbv, sbz, cum)


def _outproj_kernel(hml_ref, hsb_ref, x_ref, gate_ref, gpost_ref, wml_ref, wsb_ref, o_ref):
    hsb = jnp.concatenate([hsb_ref[0, p] for p in range(SB_PAIRS)], axis=1)
    y = (jnp.dot(hml_ref[0], wml_ref[...], preferred_element_type=F32)
         + jnp.dot(hsb, wsb_ref[...], preferred_element_type=F32))
    r = lax.rsqrt(jnp.mean(y * y, axis=-1, keepdims=True) + EPS)
    o_ref[0] = x_ref[0] + gate_ref[0] * ((y * r) * gpost_ref[...])


def _outproj(hml, hsb, x, mod3, g_post, w_oml, w_osb):
    bsz, s, d = x.shape
    tm = min(OUT_TM, s)
    return pl.pallas_call(
        _outproj_kernel,
        grid=(bsz, s // tm),
        in_specs=[pl.BlockSpec((1, tm, ML_WIDTH), lambda b, i: (b, i, 0)),
                  pl.BlockSpec((1, SB_PAIRS, tm, LANES), lambda b, i: (b, 0, i, 0)),
                  pl.BlockSpec((1, tm, d), lambda b, i: (b, i, 0)),
                  pl.BlockSpec((1, 1, d), lambda b, i: (b, 0, 2)),
                  pl.BlockSpec((1, d), lambda b, i: (0, 0)),
                  pl.BlockSpec((ML_WIDTH, d), lambda b, i: (0, 0)),
                  pl.BlockSpec((SB_WIDTH, d), lambda b, i: (0, 0))],
        out_specs=pl.BlockSpec((1, tm, d), lambda b, i: (b, i, 0)),
        out_shape=jax.ShapeDtypeStruct((bsz, s, d), F32),
        compiler_params=pltpu.CompilerParams(
            dimension_semantics=("parallel", "parallel"), vmem_limit_bytes=VMEM_LIMIT),
        name="outproj",
    )(hml, hsb, x, mod3, g_post.reshape(1, d), w_oml, w_osb)


def kernel(x, c, w_ada, b_ada, g_pre, w_in, b_igate, b_fgate, conv_w, conv_b, g_ml_head, w_out, g_post):
    bsz, s, d = x.shape
    o_gate = 2 * ML_WIDTH + 3 * ML_WIDTH
    o_sb = o_gate + 2 * ML_HEADS
    w_ml = w_in[:, :o_gate].astype(BF16)
    sb_q, sb_k, sb_v, sb_z = (w_in[:, o_sb + n * SB_WIDTH:o_sb + (n + 1) * SB_WIDTH] for n in range(4))
    w_sb = jnp.concatenate([sb_q * (LOG2E * SB_HEAD_DIM ** -0.5), sb_v, sb_z], axis=1).astype(BF16)
    w_kt = jnp.pad(jnp.concatenate([sb_k, w_in[:, o_gate:o_sb]], axis=1).T,
                   ((0, 2 * SUBLANES - 2 * ML_HEADS), (0, 0))).astype(BF16)
    gbias = jnp.pad(jnp.concatenate([b_igate, b_fgate]), (0, LANES - 2 * ML_HEADS)).reshape(1, LANES)
    w_o = w_out.astype(BF16)

    mod3 = _adaln(c, w_ada, b_ada).reshape(bsz, 1, 3 * d)
    ml, sbq, sbk, sbv, sbz, gates = _inproj(x, mod3, g_pre, w_ml, w_kt, w_sb)
    hml = _mlstm(ml, gates, conv_w, conv_b, gbias, g_ml_head)
    hsb = _stick_breaking(sbq, sbk, sbv, sbz)
    return _outproj(hml, hsb, x, mod3, g_post, w_o[:ML_WIDTH], w_o[ML_WIDTH:])
```

```python
import functools

import jax
import jax.numpy as jnp
from jax import lax
from jax.experimental import pallas as pl
from jax.experimental.pallas import tpu as pltpu

F32 = jnp.float32
BF16 = jnp.bfloat16
HIGHEST = lax.Precision.HIGHEST

EPS = 1e-6
ML_HEADS = 4
ML_HEAD_DIM = 128
ML_WIDTH = ML_HEADS * ML_HEAD_DIM
SB_HEADS = 8
SB_HEAD_DIM = 64
SB_WIDTH = SB_HEADS * SB_HEAD_DIM
SB_PAIRS = SB_HEADS // 2
CONV_K = 4
CHUNK = 64
LANES = 128
SUBLANES = 8

IN_TM = 512
ML_T = 512
SB_TQ = 128
SB_TK = 128
SB_ROWS = 256
OUT_TM = 2048
LOG2E = 1.4426950408889634
SB_SKIP = 128.0
SB_DEAD = 1e30
SB_MIN_SWEEPS = 3
SB_ZMAX = 126.0
VMEM_LIMIT = 56 * 1024 * 1024


def _sigmoid(x):
    return 0.5 + 0.5 * jnp.tanh(0.5 * x)


def _silu(x):
    hx = 0.5 * x
    return hx + hx * jnp.tanh(hx)


def _softplus(x):
    return jnp.maximum(x, 0.0) + jnp.log(1.0 + jnp.exp(-jnp.abs(x)))


def _adaln_kernel(c_ref, w_ref, b_ref, o_ref):
    c = c_ref[...]
    s = _silu(c)
    o_ref[...] = jnp.dot(s, w_ref[...], preferred_element_type=F32, precision=HIGHEST) + b_ref[...]


def _adaln(c, w_ada, b_ada):
    bsz, d = c.shape
    n = w_ada.shape[1]
    return pl.pallas_call(
        _adaln_kernel,
        grid=(n // d,),
        in_specs=[pl.BlockSpec((bsz, d), lambda j: (0, 0)),
                  pl.BlockSpec((d, d), lambda j: (0, j)),
                  pl.BlockSpec((1, d), lambda j: (0, j))],
        out_specs=pl.BlockSpec((bsz, d), lambda j: (0, j)),
        out_shape=jax.ShapeDtypeStruct((bsz, n), F32),
        name="adaln",
    )(c, w_ada, b_ada.reshape(1, n))


def _inproj_kernel(x_ref, shift_ref, scale_ref, gpre_ref, wml_ref, wkt_ref, wsb_ref,
                   ml_ref, sbq_ref, sbk_ref, sbv_ref, sbz_ref, gates_ref, h_scr):
    x = x_ref[0]
    r = lax.rsqrt(jnp.mean(x * x, axis=-1, keepdims=True) + EPS)
    a = gpre_ref[...] * (1.0 + scale_ref[0])
    h_scr[...] = ((x * r) * a + shift_ref[0]).astype(BF16)

    n_ml = wml_ref.shape[1]
    for n in range(n_ml // 512):
        cols = slice(n * 512, (n + 1) * 512)
        ml_ref[0, :, cols] = jnp.dot(h_scr[...], wml_ref[:, cols],
                                     preferred_element_type=F32).astype(BF16)
    tm = x.shape[0]
    kt = lax.dot_general(wkt_ref[...], h_scr[...], (((1,), (1,)), ((), ())),
                         preferred_element_type=F32)
    gates_ref[0] = kt[SB_WIDTH:SB_WIDTH + SUBLANES, :]
    kt = kt[:SB_WIDTH].astype(BF16)
    d_a = lax.broadcasted_iota(jnp.int32, (LANES, tm), 0) < SB_HEAD_DIM
    for p in range(SB_PAIRS):
        blk = kt[p * LANES:(p + 1) * LANES, :]
        zero = jnp.zeros_like(blk)
        only_a = jnp.where(d_a, blk, zero)
        only_b = jnp.where(d_a, zero, blk)
        for j in range(tm // SB_TK):
            cols = slice(j * SB_TK, (j + 1) * SB_TK)
            sbk_ref[0, p, j, :, 0:SB_TK] = only_a[:, cols]
            sbk_ref[0, p, j, :, SB_TK:2 * SB_TK] = only_b[:, cols]
    half = lax.broadcasted_iota(jnp.int32, (tm, LANES), 1) < SB_HEAD_DIM
    for n, dst in enumerate((sbq_ref, sbv_ref, sbz_ref)):
        cols = slice(n * SB_WIDTH, (n + 1) * SB_WIDTH)
        res = jnp.dot(h_scr[...], wsb_ref[:, cols], preferred_element_type=F32).astype(BF16)
        for p in range(SB_PAIRS):
            blk = res[:, p * LANES:(p + 1) * LANES]
            if dst is sbv_ref:
                zero = jnp.zeros_like(blk)
                only_a = jnp.where(half, blk, zero)
                only_b = jnp.where(half, zero, blk)
                for j in range(tm // SB_TK):
                    rows = slice(j * SB_TK, (j + 1) * SB_TK)
                    dst[0, p, 2 * j * SB_TK:(2 * j + 1) * SB_TK, :] = only_a[rows]
                    dst[0, p, (2 * j + 1) * SB_TK:(2 * j + 2) * SB_TK, :] = only_b[rows]
            else:
                dst[0, p] = blk


def _inproj(x, mod3, g_pre, w_ml, w_kt, w_sb):
    bsz, s, d = x.shape
    tm = min(IN_TM, s)
    n_ml = w_ml.shape[1]
    sb_shape = jax.ShapeDtypeStruct((bsz, SB_PAIRS, s, LANES), BF16)
    sb_spec = pl.BlockSpec((1, SB_PAIRS, tm, LANES), lambda b, i: (b, 0, i, 0))
    k_shape = jax.ShapeDtypeStruct((bsz, SB_PAIRS, s // SB_TK, LANES, 2 * SB_TK), BF16)
    k_spec = pl.BlockSpec((1, SB_PAIRS, tm // SB_TK, LANES, 2 * SB_TK), lambda b, i: (b, 0, i, 0, 0))
    v_shape = jax.ShapeDtypeStruct((bsz, SB_PAIRS, 2 * s, LANES), BF16)
    v_spec = pl.BlockSpec((1, SB_PAIRS, 2 * tm, LANES), lambda b, i: (b, 0, i, 0))
    return pl.pallas_call(
        _inproj_kernel,
        grid=(bsz, s // tm),
        in_specs=[pl.BlockSpec((1, tm, d), lambda b, i: (b, i, 0)),
                  pl.BlockSpec((1, 1, d), lambda b, i: (b, 0, 0)),
                  pl.BlockSpec((1, 1, d), lambda b, i: (b, 0, 1)),
                  pl.BlockSpec((1, d), lambda b, i: (0, 0)),
                  pl.BlockSpec((d, n_ml), lambda b, i: (0, 0)),
                  pl.BlockSpec(w_kt.shape, lambda b, i: (0, 0)),
                  pl.BlockSpec((d, 3 * SB_WIDTH), lambda b, i: (0, 0))],
        out_specs=[pl.BlockSpec((1, tm, n_ml), lambda b, i: (b, i, 0)),
                   sb_spec, k_spec, v_spec, sb_spec,
                   pl.BlockSpec((1, SUBLANES, tm), lambda b, i: (b, 0, i))],
        out_shape=[jax.ShapeDtypeStruct((bsz, s, n_ml), BF16),
                   sb_shape, k_shape, v_shape, sb_shape,
                   jax.ShapeDtypeStruct((bsz, SUBLANES, s), F32)],
        scratch_shapes=[pltpu.VMEM((tm, d), BF16)],
        compiler_params=pltpu.CompilerParams(
            dimension_semantics=("parallel", "parallel"), vmem_limit_bytes=VMEM_LIMIT),
        name="inproj",
    )(x, mod3, mod3, g_pre.reshape(1, d), w_ml, w_kt, w_sb)


def _mlstm_kernel(u_ref, v_ref, o_ref, z_ref, g_ref, convw_ref, convb_ref, gbias_ref,
                  ghead_ref, out_ref, ubuf, q_scr, k_scr, bc_scr, qkm_scr, cl_scr, nl_scr, cp_scr,
                  hg_scr, h2_scr, c_scr, n_scr, m_scr):
    t = u_ref.shape[1]
    n_chunks = t // CHUNK
    units = [(c, h) for c in range(n_chunks) for h in range(ML_HEADS)]
    j = pl.program_id(1)

    @pl.when(j == 0)
    def _():
        ubuf[...] = jnp.zeros_like(ubuf)
        c_scr[...] = jnp.zeros_like(c_scr)
        n_scr[...] = jnp.zeros_like(n_scr)
        m_scr[...] = jnp.zeros_like(m_scr)

    u = u_ref[0].astype(F32)
    first_row = lax.broadcasted_iota(jnp.int32, (SUBLANES, u.shape[1]), 0) == 0
    part = convw_ref[0:1, :] * u
    for i in range(1, CONV_K):
        carry = ubuf[i - 1:i, :]
        ubuf[i - 1:i, :] = part[t - 1:t, :]
        delayed = pltpu.roll(part, 1, 0)
        delayed = jnp.concatenate([jnp.where(first_row, carry, delayed[:SUBLANES]),
                                   delayed[SUBLANES:]], axis=0)
        part = convw_ref[i:i + 1, :] * u + delayed
    qk = _silu(part + convb_ref[...])
    q_scr[...] = qk[:, :ML_WIDTH].astype(BF16)
    k_scr[...] = qk[:, ML_WIDTH:] * (ML_HEAD_DIM ** -0.5)

    gates = jnp.concatenate([g_ref[0], jnp.zeros((LANES - SUBLANES, t), F32)], axis=0).T
    gates = gates + gbias_ref[...]
    pos = lax.broadcasted_iota(jnp.int32, (t, LANES), 0) & (CHUNK - 1)
    bcum = -_softplus(-gates)
    step = 1
    while step < CHUNK:
        bcum = bcum + jnp.where(pos >= step, pltpu.roll(bcum, step, 0), 0.0)
        step *= 2
    bcum = pltpu.roll(bcum, LANES - ML_HEADS, 1)
    cdiff = gates - bcum
    cm = cdiff
    step = 1
    while step < CHUNK:
        cm = jnp.maximum(cm, jnp.where(pos >= step, pltpu.roll(cm, step, 0), -jnp.inf))
        step *= 2
    c_t = cdiff.T[0:SUBLANES, :]

    m_run = m_scr[0:1, :]
    s_old, s_loc, stats = [], [], []
    for c in range(n_chunks):
        rows = slice(c * CHUNK, (c + 1) * CHUNK)
        b_c = bcum[rows]
        a_c = b_c[CHUNK - 1:CHUNK, :]
        g_c = a_c - b_c + gates[rows]
        m_loc = jnp.max(g_c, axis=0, keepdims=True)
        mm = jnp.maximum(m_run, cm[rows])
        stats.append((mm, jnp.exp(m_run - mm), jnp.exp(-(b_c + mm)), jnp.exp(g_c - m_loc)))
        m_new = jnp.maximum(a_c + m_run, m_loc)
        s_old.append(jnp.exp(a_c + m_run - m_new))
        s_loc.append(jnp.exp(m_loc - m_new))
        m_run = m_new
    m_scr[0:1, :] = m_run

    for u, (c, h) in enumerate(units):
        for i, stat in enumerate(stats[c]):
            bc_scr[u, i] = jnp.broadcast_to(stat[:, h:h + 1], (CHUNK, LANES))

    causal = (lax.broadcasted_iota(jnp.int32, (CHUNK, CHUNK), 0)
              >= lax.broadcasted_iota(jnp.int32, (CHUNK, CHUNK), 1))
    ones_rows = jnp.ones((2 * SUBLANES, CHUNK), BF16)

    for u, (c, h) in enumerate(units):
        rows = slice(c * CHUNK, (c + 1) * CHUNK)
        lanes = slice(h * ML_HEAD_DIM, (h + 1) * ML_HEAD_DIM)
        k_c = k_scr[rows, lanes]
        s_qk = lax.dot_general(q_scr[rows, lanes], k_c.astype(BF16), (((1,), (1,)), ((), ())),
                               preferred_element_type=F32)
        dw = jnp.where(causal, jnp.exp(c_t[h:h + 1, rows] - bc_scr[u, 0][:, :CHUNK]), 0.0)
        qkm_scr[u] = (s_qk * dw).astype(BF16)
        wk = (bc_scr[u, 3] * k_c).astype(BF16)
        cl_scr[u] = lax.dot_general(v_ref[0, rows, lanes], wk, (((0,), (0,)), ((), ())),
                                    preferred_element_type=F32)
        nl_scr[u] = jnp.dot(ones_rows, wk, preferred_element_type=F32)[:SUBLANES]

    for h in range(ML_HEADS):
        c_run = c_scr[h]
        n_run = n_scr[h:h + 1, :]
        for c in range(n_chunks):
            u = c * ML_HEADS + h
            cp_scr[u, 0:ML_HEAD_DIM, :] = c_run.astype(BF16)
            cp_scr[u, ML_HEAD_DIM:2 * ML_HEAD_DIM, :] = jnp.broadcast_to(
                n_run, (ML_HEAD_DIM, ML_HEAD_DIM)).astype(BF16)
            so = s_old[c][:, h:h + 1]
            sl = s_loc[c][:, h:h + 1]
            c_run = so * c_run + sl * cl_scr[u]
            n_run = so * n_run + sl * nl_scr[u, 0:1, :]
        c_scr[h] = c_run
        n_scr[h:h + 1, :] = n_run

    ones_cols = jnp.ones((CHUNK, ML_HEAD_DIM), BF16)
    for u, (c, h) in enumerate(units):
        rows = slice(c * CHUNK, (c + 1) * CHUNK)
        lanes = slice(h * ML_HEAD_DIM, (h + 1) * ML_HEAD_DIM)
        qc = lax.dot_general(q_scr[rows, lanes], cp_scr[u], (((1,), (1,)), ((), ())),
                             preferred_element_type=F32)
        kv = jnp.dot(qkm_scr[u], jnp.concatenate([v_ref[0, rows, lanes], ones_cols], axis=1),
                     preferred_element_type=F32)
        inter_w = bc_scr[u, 1]
        den = inter_w * qc[:, ML_HEAD_DIM:] + kv[:, ML_HEAD_DIM:]
        num = inter_w * qc[:, :ML_HEAD_DIM] + kv[:, :ML_HEAD_DIM]
        hh = num * (1.0 / jnp.maximum(jnp.abs(den), bc_scr[u, 2]))
        hg = _sigmoid(o_ref[0, rows, lanes].astype(F32)) * hh
        hg_scr[u] = hg
        h2_scr[u] = (hg * hg).astype(BF16)

    ones_sq = jnp.ones((ML_HEAD_DIM, ML_HEAD_DIM), BF16)
    for u, (c, h) in enumerate(units):
        rows = slice(c * CHUNK, (c + 1) * CHUNK)
        lanes = slice(h * ML_HEAD_DIM, (h + 1) * ML_HEAD_DIM)
        ss = jnp.dot(h2_scr[u], ones_sq, preferred_element_type=F32)
        hn = hg_scr[u] * lax.rsqrt(ss * (1.0 / ML_HEAD_DIM) + EPS) * ghead_ref[:, lanes]
        out_ref[0, rows, lanes] = (hn * _silu(z_ref[0, rows, lanes].astype(F32))).astype(BF16)


def _mlstm(ml, gates, conv_w, conv_b, gbias, g_head):
    bsz, s, _ = ml.shape
    t = min(ML_T, s)
    w2 = 2 * ML_WIDTH
    n_units = (t // CHUNK) * ML_HEADS
    return pl.pallas_call(
        _mlstm_kernel,
        grid=(bsz, s // t),
        in_specs=[pl.BlockSpec((1, t, w2), lambda b, j: (b, j, 0)),
                  pl.BlockSpec((1, t, ML_WIDTH), lambda b, j: (b, j, 2)),
                  pl.BlockSpec((1, t, ML_WIDTH), lambda b, j: (b, j, 3)),
                  pl.BlockSpec((1, t, ML_WIDTH), lambda b, j: (b, j, 4)),
                  pl.BlockSpec((1, SUBLANES, t), lambda b, j: (b, 0, j)),
                  pl.BlockSpec((CONV_K, w2), lambda b, j: (0, 0)),
                  pl.BlockSpec((1, w2), lambda b, j: (0, 0)),
                  pl.BlockSpec((1, LANES), lambda b, j: (0, 0)),
                  pl.BlockSpec((1, ML_WIDTH), lambda b, j: (0, 0))],
        out_specs=pl.BlockSpec((1, t, ML_WIDTH), lambda b, j: (b, j, 0)),
        out_shape=jax.ShapeDtypeStruct((bsz, s, ML_WIDTH), BF16),
        scratch_shapes=[pltpu.VMEM((SUBLANES, w2), F32),
                        pltpu.VMEM((t, ML_WIDTH), BF16),
                        pltpu.VMEM((t, ML_WIDTH), F32),
                        pltpu.VMEM((n_units, 4, CHUNK, LANES), F32),
                        pltpu.VMEM((n_units, CHUNK, CHUNK), BF16),
                        pltpu.VMEM((n_units, ML_HEAD_DIM, ML_HEAD_DIM), F32),
                        pltpu.VMEM((n_units, SUBLANES, ML_HEAD_DIM), F32),
                        pltpu.VMEM((n_units, 2 * ML_HEAD_DIM, ML_HEAD_DIM), BF16),
                        pltpu.VMEM((n_units, CHUNK, ML_HEAD_DIM), F32),
                        pltpu.VMEM((n_units, CHUNK, ML_HEAD_DIM), BF16),
                        pltpu.VMEM((ML_HEADS, ML_HEAD_DIM, ML_HEAD_DIM), F32),
                        pltpu.VMEM((SUBLANES, ML_HEAD_DIM), F32),
                        pltpu.VMEM((SUBLANES, LANES), F32)],
        compiler_params=pltpu.CompilerParams(
            dimension_semantics=("parallel", "arbitrary"), vmem_limit_bytes=VMEM_LIMIT),
        name="mlstm",
    )(ml, ml, ml, ml, gates, conv_w, conv_b.reshape(1, w2), gbias, g_head.reshape(1, ML_WIDTH))


def _sb_sweeps(ds, first, qt0, chains, q_ref, k_ref, v_ref, cum_ref, z_scr, n_scr, a_scr, s_scr,
               acc_scr, r_scr):
    n_chain = len(chains)
    items = [(i * n_chain + c, d, first and i == 0, g, p)
             for i, d in enumerate(ds) for c, (g, p) in enumerate(chains)]
    if first:
        strict = ((lax.broadcasted_iota(jnp.int32, (SB_TQ, 2 * SB_TK), 1) & (SB_TK - 1))
                  < lax.broadcasted_iota(jnp.int32, (SB_TQ, 2 * SB_TK), 0))
    lane_a = lax.broadcasted_iota(jnp.int32, (SB_TQ, LANES), 1) < SB_HEAD_DIM

    def key_rows(g, d):
        return pl.ds(jnp.maximum(qt0 + g - d, 0) * (2 * SB_TK), 2 * SB_TK)

    def item_rows(slot):
        return slice(slot * SB_TQ, (slot + 1) * SB_TQ)

    for slot, d, diag, g, p in items:
        q2 = q_ref[0, p, g * SB_TQ:(g + 1) * SB_TQ, :]
        z = jnp.dot(q2, k_ref[0, p, jnp.maximum(qt0 + g - d, 0)],
                    preferred_element_type=F32)
        nlk = jnp.log(1.0 + jnp.exp2(jnp.minimum(z, SB_ZMAX))) * LOG2E
        if diag:
            nlk = jnp.where(strict, nlk, 0.0)
        z_scr[slot] = z
        n_scr[item_rows(slot), :] = nlk.astype(BF16)

    m_all = jnp.dot(n_scr[0:len(items) * SB_TQ, :], cum_ref[...], preferred_element_type=F32)
    for slot, d, diag, g, p in items:
        m = m_all[item_rows(slot), 0:2 * SB_TK]
        a = jnp.exp2(z_scr[slot] - m)
        if diag:
            a = jnp.where(strict, a, 0.0)
        a_scr[slot] = a.astype(BF16)
        s_scr[slot] = m_all[item_rows(slot), 2 * SB_TK:]

    rmin = None
    for c, (g, p) in enumerate(chains):
        acc, r = (None, None) if first else (acc_scr[c], r_scr[c])
        for i, d in enumerate(ds):
            slot = i * n_chain + c
            pv = jnp.dot(a_scr[slot], v_ref[0, p, key_rows(g, d), :], preferred_element_type=F32)
            if first and i == 0:
                acc, r = pv, s_scr[slot]
            else:
                r = jnp.where(qt0 + g - d >= 0, r, SB_DEAD)
                acc = acc + jnp.exp2(-r) * pv
                r = r + s_scr[slot]
        acc_scr[c] = acc
        r_scr[c] = r
        rmin = r if rmin is None else jnp.minimum(rmin, r)
    return jnp.min(rmin)


def _sb_kernel(q_ref, k_ref, v_ref, z_ref, cum_ref, o_ref, z_scr, n_scr, a_scr, s_scr, acc_scr, r_scr):
    n_sub = q_ref.shape[2] // SB_TQ
    qt0 = pl.program_id(1) * n_sub
    chains = [(g, p) for g in range(n_sub) for p in range(SB_PAIRS)]
    sweeps = functools.partial(_sb_sweeps, qt0=qt0, chains=chains, q_ref=q_ref, k_ref=k_ref,
                               v_ref=v_ref, cum_ref=cum_ref, z_scr=z_scr, n_scr=n_scr, a_scr=a_scr,
                               s_scr=s_scr, acc_scr=acc_scr, r_scr=r_scr)

    def cond(carry):
        d, rm = carry
        return jnp.logical_and(qt0 + (n_sub - 1) - d >= 0, rm < SB_SKIP)

    def body(carry):
        d, _ = carry
        return d + 1, sweeps([d], False)

    rm = sweeps(list(range(SB_MIN_SWEEPS)), True)
    lax.while_loop(cond, body, (jnp.int32(SB_MIN_SWEEPS), rm))

    for c, (g, p) in enumerate(chains):
        zc = z_ref[0, p, g * SB_TQ:(g + 1) * SB_TQ, :].astype(F32)
        o_ref[0, p, g * SB_TQ:(g + 1) * SB_TQ, :] = (acc_scr[c] * _silu(zc)).astype(BF16)


def _stick_breaking(sbq, sbk, sbv, sbz):
    bsz, pairs, s, _ = sbq.shape
    rows = min(SB_ROWS, s)
    j = lax.broadcasted_iota(jnp.int32, (2 * SB_TK, 2 * SB_TK), 0)
    c = lax.broadcasted_iota(jnp.int32, (2 * SB_TK, 2 * SB_TK), 1)
    cum = jnp.where((j >= c) & ((j // SB_TK) == (c // SB_TK)), 1.0, 0.0)
    jr = lax.broadcasted_iota(jnp.int32, (2 * SB_TK, LANES), 0)
    cr = lax.broadcasted_iota(jnp.int32, (2 * SB_TK, LANES), 1)
    tot = jnp.where((jr // SB_TK) == (cr // SB_HEAD_DIM), 1.0, 0.0)
    cum = jnp.concatenate([cum, tot], axis=1).astype(BF16)
    q_spec = pl.BlockSpec((1, pairs, rows, LANES), lambda b, i: (b, 0, i, 0))
    k_spec = pl.BlockSpec((1, pairs, s // SB_TK, LANES, 2 * SB_TK), lambda b, i: (b, 0, 0, 0, 0))
    v_spec = pl.BlockSpec((1, pairs, 2 * s, LANES), lambda b, i: (b, 0, 0, 0))
    n_chain = (rows // SB_TQ) * pairs
    n_item = SB_MIN_SWEEPS * n_chain
    return pl.pallas_call(
        _sb_kernel,
        grid=(bsz, s // rows),
        in_specs=[q_spec, k_spec, v_spec, q_spec,
                  pl.BlockSpec((2 * SB_TK, 2 * SB_TK + LANES), lambda b, i: (0, 0))],
        out_specs=q_spec,
        out_shape=jax.ShapeDtypeStruct((bsz, pairs, s, LANES), BF16),
        scratch_shapes=[pltpu.VMEM((n_item, SB_TQ, 2 * SB_TK), F32),
                        pltpu.VMEM((n_item * SB_TQ, 2 * SB_TK), BF16),
                        pltpu.VMEM((n_item, SB_TQ, 2 * SB_TK), BF16),
                        pltpu.VMEM((n_item, SB_TQ, LANES), F32),
                        pltpu.VMEM((n_chain, SB_TQ, LANES), F32),
                        pltpu.VMEM((n_chain, SB_TQ, LANES), F32)],
        compiler_params=pltpu.CompilerParams(
            dimension_semantics=("parallel", "arbitrary"), vmem_limit_bytes=VMEM_LIMIT),
        name="stickbreak",
    )(sbq, sbk, sbv, sbz, cum)


def _outproj_kernel(hml_ref, hsb_ref, x_ref, gate_ref, gpost_ref, wml_ref, wsb_ref, o_ref):
    hsb = jnp.concatenate([hsb_ref[0, p] for p in range(SB_PAIRS)], axis=1)
    y = (jnp.dot(hml_ref[0], wml_ref[...], preferred_element_type=F32)
         + jnp.dot(hsb, wsb_ref[...], preferred_element_type=F32))
    r = lax.rsqrt(jnp.mean(y * y, axis=-1, keepdims=True) + EPS)
    o_ref[0] = x_ref[0] + gate_ref[0] * ((y * r) * gpost_ref[...])


def _outproj(hml, hsb, x, mod3, g_post, w_oml, w_osb):
    bsz, s, d = x.shape
    tm = min(OUT_TM, s)
    return pl.pallas_call(
        _outproj_kernel,
        grid=(bsz, s // tm),
        in_specs=[pl.BlockSpec((1, tm, ML_WIDTH), lambda b, i: (b, i, 0)),
                  pl.BlockSpec((1, SB_PAIRS, tm, LANES), lambda b, i: (b, 0, i, 0)),
                  pl.BlockSpec((1, tm, d), lambda b, i: (b, i, 0)),
                  pl.BlockSpec((1, 1, d), lambda b, i: (b, 0, 2)),
                  pl.BlockSpec((1, d), lambda b, i: (0, 0)),
                  pl.BlockSpec((ML_WIDTH, d), lambda b, i: (0, 0)),
                  pl.BlockSpec((SB_WIDTH, d), lambda b, i: (0, 0))],
        out_specs=pl.BlockSpec((1, tm, d), lambda b, i: (b, i, 0)),
        out_shape=jax.ShapeDtypeStruct((bsz, s, d), F32),
        compiler_params=pltpu.CompilerParams(
            dimension_semantics=("parallel", "parallel"), vmem_limit_bytes=VMEM_LIMIT),
        name="outproj",
    )(hml, hsb, x, mod3, g_post.reshape(1, d), w_oml, w_osb)


def kernel(x, c, w_ada, b_ada, g_pre, w_in, b_igate, b_fgate, conv_w, conv_b, g_ml_head, w_out, g_post):
    bsz, s, d = x.shape
    o_gate = 2 * ML_WIDTH + 3 * ML_WIDTH
    o_sb = o_gate + 2 * ML_HEADS
    w_ml = w_in[:, :o_gate].astype(BF16)
    sb_q, sb_k, sb_v, sb_z = (w_in[:, o_sb + n * SB_WIDTH:o_sb + (n + 1) * SB_WIDTH] for n in range(4))
    w_sb = jnp.concatenate([sb_q * (LOG2E * SB_HEAD_DIM ** -0.5), sb_v, sb_z], axis=1).astype(BF16)
    w_kt = jnp.pad(jnp.concatenate([sb_k, w_in[:, o_gate:o_sb]], axis=1).T,
                   ((0, 2 * SUBLANES - 2 * ML_HEADS), (0, 0))).astype(BF16)
    gbias = jnp.pad(jnp.concatenate([b_igate, b_fgate]), (0, LANES - 2 * ML_HEADS)).reshape(1, LANES)
    w_o = w_out.astype(BF16)

    mod3 = _adaln(c, w_ada, b_ada).reshape(bsz, 1, 3 * d)
    ml, sbq, sbk, sbv, sbz, gates = _inproj(x, mod3, g_pre, w_ml, w_kt, w_sb)
    hml = _mlstm(ml, gates, conv_w, conv_b, gbias, g_ml_head)
    hsb = _stick_breaking(sbq, sbk, sbv, sbz)
    return _outproj(hml, hsb, x, mod3, g_post, w_o[:ML_WIDTH], w_o[ML_WIDTH:])
```
